```python
import math
import jax, jax.numpy as jnp
from jax import lax
import numpy as np

D_MODEL = 4096
BATCH = 4
SEQ = 4096
DEPTH = 2

ATTN_PATTERNS = ((128, 1), (512, 4), (2048, 16))
N_GROUPS_ATTN = 3
HEADS_PER_GROUP = 16
HEAD_DIM = 128
D_ATTN = HEADS_PER_GROUP * HEAD_DIM
ATTN_BLOCK = 128
QKV_COLS = N_GROUPS_ATTN * 3 * D_ATTN
IN_ATTN_COLS = QKV_COLS + D_ATTN

NUM_BUCKETS = 32
MAX_DISTANCE = 2048
N_BIAS_HEADS = N_GROUPS_ATTN * HEADS_PER_GROUP

EXPAND = 2
D_INNER = EXPAND * D_MODEL
SSM_HEAD_DIM = 64
SSM_HEADS = D_INNER // SSM_HEAD_DIM
SSM_GROUPS = 8
HEADS_PER_SSM_GROUP = SSM_HEADS // SSM_GROUPS
D_STATE = 128
CONV_WIDTH = 4
CONV_DIM = D_INNER + 2 * SSM_GROUPS * D_STATE
IN_SSM_COLS = D_INNER + CONV_DIM + SSM_HEADS
CHUNK = 128

N_MIXERS = 2
N_ATTN_LAYERS = (DEPTH + 1) // 2
N_SSM_LAYERS = DEPTH // 2
DEEPNORM_ALPHA = (2 * DEPTH) ** 0.25
DEEPNORM_BETA = (8 * DEPTH) ** -0.25
LN_EPS = 1e-5
RMS_EPS = 1e-5
NEG_INF = -1e30

kernel_name = 'hybrid_dilated_attn_mamba2_deepnorm'


def t5_causal_bucket(dist):
    max_exact = NUM_BUCKETS // 2
    d_f = jnp.maximum(dist, 1).astype(jnp.float32)
    large = max_exact + (jnp.log(d_f / max_exact) / math.log(MAX_DISTANCE / max_exact)
                         * (NUM_BUCKETS - max_exact)).astype(jnp.int32)
    large = jnp.minimum(large, NUM_BUCKETS - 1)
    return jnp.where(dist < max_exact, dist, large)


def dilated_group_attention(q, k, v, bias_table, window, dilation):
    b, s, h, dh = q.shape
    span = window // dilation
    L = s // dilation
    nb = -(-L // ATTN_BLOCK)
    lp = nb * ATTN_BLOCK

    def to_sub(t):
        t = t.reshape(b, L, dilation, h, dh).transpose(0, 2, 1, 3, 4)
        t = jnp.pad(t, ((0, 0), (0, 0), (0, lp - L), (0, 0), (0, 0)))
        return t.reshape(b, dilation, nb, ATTN_BLOCK, h, dh)

    def with_prev(t):
        prev = jnp.pad(t, ((0, 0), (0, 0), (1, 0), (0, 0), (0, 0), (0, 0)))[:, :, :-1]
        return jnp.concatenate([prev, t], axis=3)

    qb = to_sub(q)
    kk = with_prev(to_sub(k))
    vv = with_prev(to_sub(v))

    qi = jnp.arange(ATTN_BLOCK)[:, None]
    ki = jnp.arange(2 * ATTN_BLOCK)[None, :]
    delta = ATTN_BLOCK + qi - ki
    band = (delta >= 0) & (delta <= span)
    not_first = (jnp.arange(nb) > 0)[:, None, None]
    valid = band[None] & (not_first | (ki >= ATTN_BLOCK)[None])
    bucket = t5_causal_bucket(jnp.clip(delta, 0, None) * dilation)
    bias = bias_table.astype(jnp.float32)[bucket].transpose(2, 0, 1)

    logits = jnp.einsum('brnqhd,brnkhd->brnhqk', qb, kk).astype(jnp.float32)
    logits = logits * (dh ** -0.5) + bias[None, None, None]
    logits = jnp.where(valid[None, None, :, None], logits, NEG_INF)
    m = jnp.max(logits, axis=-1, keepdims=True)
    p = jnp.exp(logits - m)
    denom = jnp.sum(p, axis=-1, keepdims=True)
    o = jnp.einsum('brnhqk,brnkhd->brnqhd', p / denom, vv.astype(jnp.float32))
    lse = (m + jnp.log(denom))[..., 0]

    o = o.reshape(b, dilation, lp, h, dh)[:, :, :L].transpose(0, 2, 1, 3, 4).reshape(b, s, h, dh)
    lse = lse.transpose(0, 1, 2, 4, 3).reshape(b, dilation, lp, h)[:, :, :L]
    lse = lse.transpose(0, 2, 1, 3).reshape(b, s, h)
    return o, lse


def dilated_attention_mixer(x, w_in, w_out, rel_bias):
    b, s, _ = x.shape
    proj = jnp.einsum('bsd,de->bse', x, w_in)
    qkv = proj[..., :QKV_COLS].reshape(b, s, N_GROUPS_ATTN, 3, HEADS_PER_GROUP, HEAD_DIM)
    gate = proj[..., QKV_COLS:]
    outs, lses = [], []
    for g, (window, dilation) in enumerate(ATTN_PATTERNS):
        o, lse = dilated_group_attention(
            qkv[:, :, g, 0], qkv[:, :, g, 1], qkv[:, :, g, 2],
            rel_bias[:, g * HEADS_PER_GROUP:(g + 1) * HEADS_PER_GROUP], window, dilation)
        outs.append(o)
        lses.append(lse)
    w = jax.nn.softmax(jnp.stack(lses), axis=0)
    o = jnp.einsum('gbsh,gbshd->bshd', w, jnp.stack(outs)).reshape(b, s, D_ATTN)
    y = o.astype(x.dtype) * jax.nn.silu(gate)
    return jnp.einsum('bse,ed->bsd', y, w_out)


def ssd_chunked_scan(xs, dt, a, bm, cm):
    b, s, g, hpg, p = xs.shape
    n = bm.shape[-1]
    nc = s // CHUNK

    def chunks(t):
        return t.reshape((b, nc, CHUNK) + t.shape[2:]).swapaxes(0, 1)

    causal = jnp.tril(jnp.ones((CHUNK, CHUNK), dtype=bool))

    def step(state, inp):
        xc, dtc, bc, cc = inp
        bc = bc.astype(jnp.float32)
        cc = cc.astype(jnp.float32)
        a_cum = jnp.cumsum(dtc * a, axis=1)
        seg = a_cum[:, :, None] - a_cum[:, None, :]
        decay = jnp.exp(jnp.where(causal[None, :, :, None, None], seg, -jnp.inf))
        xdt = xc.astype(jnp.float32) * dtc[..., None]
        cb = jnp.einsum('blgn,bsgn->blsg', cc, bc)
        y_diag = jnp.einsum('blsg,blsgh,bsghp->blghp', cb, decay, xdt)
        y_off = jnp.einsum('blgn,bghpn,blgh->blghp', cc, state, jnp.exp(a_cum))
        to_end = jnp.exp(a_cum[:, -1:] - a_cum)
        new_state = (state * jnp.exp(a_cum[:, -1])[..., None, None]
                     + jnp.einsum('bsgn,bsgh,bsghp->bghpn', bc, to_end, xdt))
        return new_state, y_diag + y_off

    state0 = jnp.zeros((b, g, hpg, p, n), jnp.float32)
    _, y = lax.scan(step, state0, (chunks(xs), chunks(dt), chunks(bm), chunks(cm)))
    return y.swapaxes(0, 1).reshape(b, s, g, hpg, p)


def ssd_mixer(x, w_in, conv_w, conv_b, dt_bias, a_log, d_skip, norm_w, w_out):
    b, s, _ = x.shape
    proj = jnp.einsum('bsd,de->bse', x, w_in)
    z = proj[..., :D_INNER]
    xbc = proj[..., D_INNER:D_INNER + CONV_DIM]
    dt_raw = proj[..., D_INNER + CONV_DIM:]
    xbc = lax.conv_general_dilated(
        xbc, conv_w[:, None, :], window_strides=(1,), padding=[(CONV_WIDTH - 1, 0)],
        dimension_numbers=('NWC', 'WIO', 'NWC'), feature_group_count=CONV_DIM)
    xbc = jax.nn.silu(xbc + conv_b)
    gn = SSM_GROUPS * D_STATE
    xs = xbc[..., :D_INNER].reshape(b, s, SSM_GROUPS, HEADS_PER_SSM_GROUP, SSM_HEAD_DIM)
    bm = xbc[..., D_INNER:D_INNER + gn].reshape(b, s, SSM_GROUPS, D_STATE)
    cm = xbc[..., D_INNER + gn:].reshape(b, s, SSM_GROUPS, D_STATE)
    dt = jax.nn.softplus(dt_raw.astype(jnp.float32) + dt_bias.astype(jnp.float32))
    dt = dt.reshape(b, s, SSM_GROUPS, HEADS_PER_SSM_GROUP)
    a = -jnp.exp(a_log.astype(jnp.float32)).reshape(SSM_GROUPS, HEADS_PER_SSM_GROUP)
    y = ssd_chunked_scan(xs, dt, a, bm, cm)
    y = y + d_skip.astype(jnp.float32).reshape(SSM_GROUPS, HEADS_PER_SSM_GROUP)[:, :, None] * xs
    y = y.reshape(b, s, D_INNER) * jax.nn.silu(z.astype(jnp.float32))
    yg = y.reshape(b, s, SSM_GROUPS, D_INNER // SSM_GROUPS)
    yg = yg * lax.rsqrt(jnp.mean(yg * yg, axis=-1, keepdims=True) + RMS_EPS)
    y = yg.reshape(b, s, D_INNER) * norm_w.astype(jnp.float32)
    return jnp.einsum('bse,ed->bsd', y.astype(x.dtype), w_out)


def layer_norm(x, g, b):
    xf = x.astype(jnp.float32)
    mu = jnp.mean(xf, axis=-1, keepdims=True)
    var = jnp.mean(jnp.square(xf - mu), axis=-1, keepdims=True)
    return ((xf - mu) * lax.rsqrt(var + LN_EPS) * g.astype(jnp.float32)
            + b.astype(jnp.float32)).astype(x.dtype)


def setup_inputs(seed: int = 0) -> dict:
    key = jax.random.key(seed)
    ks = jax.random.split(key, 15)
    f32 = jnp.float32
    nrm = jax.random.normal
    x = nrm(ks[0], (BATCH, SEQ, D_MODEL), f32)
    w_in_attn = nrm(ks[1], (N_ATTN_LAYERS, D_MODEL, IN_ATTN_COLS), f32) * D_MODEL ** -0.5
    w_out_attn = nrm(ks[2], (N_ATTN_LAYERS, D_ATTN, D_MODEL), f32) * (D_ATTN ** -0.5 * DEEPNORM_BETA)
    rel_bias = nrm(ks[3], (NUM_BUCKETS, N_BIAS_HEADS), f32) * 0.5
    w_in_ssm = nrm(ks[4], (N_SSM_LAYERS, D_MODEL, IN_SSM_COLS), f32) * D_MODEL ** -0.5
    conv_w = nrm(ks[5], (N_SSM_LAYERS, CONV_WIDTH, CONV_DIM), f32) * CONV_WIDTH ** -0.5
    conv_b = nrm(ks[6], (N_SSM_LAYERS, CONV_DIM), f32) * 0.02
    dt0 = jnp.exp(jax.random.uniform(ks[7], (N_SSM_LAYERS, SSM_HEADS), f32,
                                     minval=math.log(1e-3), maxval=math.log(1e-1)))
    dt_bias = dt0 + jnp.log(-jnp.expm1(-dt0))
    a_log = jnp.log(jax.random.uniform(ks[8], (N_SSM_LAYERS, SSM_HEADS), f32, minval=1.0, maxval=16.0))
    d_skip = 1.0 + 0.1 * nrm(ks[9], (N_SSM_LAYERS, SSM_HEADS), f32)
    ssm_norm_w = 1.0 + 0.02 * nrm(ks[10], (N_SSM_LAYERS, D_INNER), f32)
    w_out_ssm = nrm(ks[11], (N_SSM_LAYERS, D_INNER, D_MODEL), f32) * (D_INNER ** -0.5 * DEEPNORM_BETA)
    ln_g = 1.0 + 0.02 * nrm(ks[12], (DEPTH, D_MODEL), f32)
    ln_b = 0.02 * nrm(ks[13], (DEPTH, D_MODEL), f32)
    return {'x': x, 'w_in_attn': w_in_attn, 'w_out_attn': w_out_attn, 'rel_bias': rel_bias,
            'w_in_ssm': w_in_ssm, 'conv_w': conv_w, 'conv_b': conv_b, 'dt_bias': dt_bias,
            'a_log': a_log, 'd_skip': d_skip, 'ssm_norm_w': ssm_norm_w, 'w_out_ssm': w_out_ssm,
            'ln_g': ln_g, 'ln_b': ln_b}


def reference(x, w_in_attn, w_out_attn, rel_bias, w_in_ssm, conv_w, conv_b, dt_bias,
              a_log, d_skip, ssm_norm_w, w_out_ssm, ln_g, ln_b):
    for i in range(DEPTH):
        j = i // N_MIXERS
        if i % N_MIXERS == 0:
            h = dilated_attention_mixer(x, w_in_attn[j], w_out_attn[j], rel_bias)
        else:
            h = ssd_mixer(x, w_in_ssm[j], conv_w[j], conv_b[j], dt_bias[j], a_log[j],
                          d_skip[j], ssm_norm_w[j], w_out_ssm[j])
        x = layer_norm(DEEPNORM_ALPHA * x + h, ln_g[i], ln_b[i])
    return x
```

```python
import functools
import math

import jax
import jax.numpy as jnp
from jax import lax
from jax.experimental import pallas as pl
from jax.experimental.pallas import tpu as pltpu

F32 = jnp.float32
BF16 = jnp.bfloat16

LANES = 128
VMEM_LIMIT_BYTES = 56 * 1024 * 1024

ATTN_PATTERNS = ((128, 1), (512, 4), (2048, 16))
N_GROUPS_ATTN = 3
HEADS_PER_GROUP = 16
HEAD_DIM = 128
D_ATTN = HEADS_PER_GROUP * HEAD_DIM
ATTN_BLOCK = 128
NUM_BUCKETS = 32
MAX_DISTANCE = 2048
SSM_HEAD_DIM = 64
SSM_GROUPS = 8
HEADS_PER_SSM_GROUP = 16
D_STATE = 128
CONV_WIDTH = 4
CHUNK = 128
LN_EPS = 1e-5
RMS_EPS = 1e-5
NEG_INF = -1e30

GROUP_CH = HEADS_PER_SSM_GROUP * SSM_HEAD_DIM
PAIRS = GROUP_CH // LANES
CARRY_ROWS = 8


def _cparams(*sem):
    return pltpu.CompilerParams(dimension_semantics=sem, vmem_limit_bytes=VMEM_LIMIT_BYTES)


def _proj_planes_kernel(x_ref, w_ref, o_ref):
    acc = jnp.dot(x_ref[...], w_ref[...], preferred_element_type=F32)
    for c in range(o_ref.shape[0]):
        o_ref[c] = acc[:, c * LANES:(c + 1) * LANES].astype(o_ref.dtype)


def proj_planes(xb, w, dilation, bm, bn):
    b, s, d_model = xb.shape
    n = w.shape[1]
    sub_len = s // dilation
    bm = min(bm, sub_len)
    nbl = sub_len // bm
    xv = xb.reshape(b, sub_len, dilation * d_model)
    grid = (n // bn, b, dilation, nbl)
    return pl.pallas_call(
        _proj_planes_kernel,
        grid=grid,
        in_specs=[
            pl.BlockSpec((None, bm, d_model), lambda j, bi, r, i: (bi, i, r)),
            pl.BlockSpec((d_model, bn), lambda j, bi, r, i: (0, j)),
        ],
        out_specs=pl.BlockSpec(
            (bn // LANES, bm, LANES),
            lambda j, bi, r, i: (j, (bi * dilation + r) * nbl + i, 0)),
        out_shape=jax.ShapeDtypeStruct((n // LANES, b * s, LANES), BF16),
        compiler_params=_cparams("parallel", "parallel", "parallel", "parallel"),
        name=f"proj_planes_d{dilation}",
    )(xv, w)


def _proj_flat_kernel(x_ref, w_ref, o_ref):
    o_ref[...] = jnp.dot(x_ref[...], w_ref[...],
                         preferred_element_type=F32).astype(o_ref.dtype)


def proj_flat(xb, w, n_out, bm, bn):
    m, d_model = xb.shape
    return pl.pallas_call(
        _proj_flat_kernel,
        grid=(n_out // bn, m // bm),
        in_specs=[
            pl.BlockSpec((bm, d_model), lambda j, i: (i, 0)),
            pl.BlockSpec((d_model, bn), lambda j, i: (0, j)),
        ],
        out_specs=pl.BlockSpec((bm, bn), lambda j, i: (i, j)),
        out_shape=jax.ShapeDtypeStruct((m, n_out), BF16),
        compiler_params=_cparams("parallel", "parallel"),
        name="proj_flat",
    )(xb, w)


def _t5_causal_bucket(dist):
    max_exact = NUM_BUCKETS // 2
    d_f = jnp.maximum(dist, 1).astype(F32)
    large = max_exact + (jnp.log(d_f / max_exact) / math.log(MAX_DISTANCE / max_exact)
                         * (NUM_BUCKETS - max_exact)).astype(jnp.int32)
    large = jnp.minimum(large, NUM_BUCKETS - 1)
    return jnp.where(dist < max_exact, dist, large)


def _band_buckets():
    qi = jnp.arange(ATTN_BLOCK)[:, None]
    ki = jnp.arange(2 * ATTN_BLOCK)[None, :]
    delta = ATTN_BLOCK + qi - ki
    out = []
    for window, dilation in ATTN_PATTERNS:
        span = window // dilation
        band = (delta >= 0) & (delta <= span)
        bucket = _t5_causal_bucket(jnp.clip(delta, 0, None) * dilation)
        out.append(jnp.where(band, bucket, -1))
    return jnp.stack(out).astype(jnp.int32)


def _bias_kernel(table_ref, bucket_ref, o_ref):
    g = pl.program_id(0)
    bucket = bucket_ref[...]

    def per_head(h, carry):
        acc = jnp.full(bucket.shape, NEG_INF, F32)
        for b in range(NUM_BUCKETS):
            acc = jnp.where(bucket == b, table_ref[b, g * HEADS_PER_GROUP + h], acc)
        o_ref[h] = acc
        return carry

    lax.fori_loop(0, HEADS_PER_GROUP, per_head, 0)


def masked_bias(rel_bias):
    buckets = _band_buckets()
    return pl.pallas_call(
        _bias_kernel,
        grid=(N_GROUPS_ATTN,),
        in_specs=[
            pl.BlockSpec(memory_space=pltpu.SMEM),
            pl.BlockSpec((None, ATTN_BLOCK, 2 * ATTN_BLOCK), lambda g: (g, 0, 0)),
        ],
        out_specs=pl.BlockSpec((None, HEADS_PER_GROUP, ATTN_BLOCK, 2 * ATTN_BLOCK),
                               lambda g: (g, 0, 0, 0)),
        out_shape=jax.ShapeDtypeStruct(
            (N_GROUPS_ATTN, HEADS_PER_GROUP, ATTN_BLOCK, 2 * ATTN_BLOCK), F32),
        compiler_params=_cparams("parallel"),
        name="masked_bias",
    )(rel_bias.astype(F32), buckets)


def _silu(v):
    return v * (1.0 / (1.0 + jnp.exp(-v)))


def _attn_kernel(q0, k0, v0, q1, k1, v1, q2, k2, v2, gate_ref, bias_ref, y_ref, o_sc, lse_sc):
    seq = y_ref.shape[0]
    blk = ATTN_BLOCK
    qkv = ((q0, k0, v0), (q1, k1, v1), (q2, k2, v2))

    def finish(g, dilation, tok0, s, vals):
        m = jnp.max(s, axis=-1, keepdims=True)
        p = jnp.exp(s - m)
        denom = jnp.sum(p, axis=-1, keepdims=True)
        acc = jnp.dot(p.astype(BF16), vals, preferred_element_type=F32)
        o = acc * (1.0 / denom)
        lse = jnp.broadcast_to(m + jnp.log(denom), (blk, LANES))
        if dilation == 1:
            rows = pl.ds(tok0, blk)
        else:
            rows = pl.ds(tok0, blk, stride=dilation)
        o_sc[g, rows, :] = o
        lse_sc[g, rows, :] = lse

    for g, (_, dilation) in enumerate(ATTN_PATTERNS):
        q_ref, k_ref, v_ref = qkv[g]
        sub_len = seq // dilation
        nb = sub_len // blk

        def per_class(r, carry, g=g, dilation=dilation, q_ref=q_ref, k_ref=k_ref,
                      v_ref=v_ref, sub_len=sub_len, nb=nb):
            base = pl.multiple_of(r * sub_len, blk)
            q = q_ref[pl.ds(base, blk), :]
            keys = k_ref[pl.ds(base, blk), :]
            s = lax.dot_general(q, keys, (((1,), (1,)), ((), ())), preferred_element_type=F32)
            s = s + bias_ref[g, :, blk:]
            finish(g, dilation, r, s, v_ref[pl.ds(base, blk), :])

            def per_block(n, c2):
                row0 = pl.multiple_of(base + n * blk, blk)
                q = q_ref[pl.ds(row0, blk), :]
                keys = k_ref[pl.ds(row0 - blk, 2 * blk), :]
                s = lax.dot_general(q, keys, (((1,), (1,)), ((), ())),
                                    preferred_element_type=F32)
                s = s + bias_ref[g]
                finish(g, dilation, r + dilation * n * blk, s,
                       v_ref[pl.ds(row0 - blk, 2 * blk), :])
                return c2

            lax.fori_loop(1, nb, per_block, 0)
            return carry

        lax.fori_loop(0, dilation, per_class, 0)

    def combine(i, carry):
        rows = pl.ds(pl.multiple_of(i * blk, blk), blk)
        l0, l1, l2 = lse_sc[0, rows, :], lse_sc[1, rows, :], lse_sc[2, rows, :]
        m = jnp.maximum(jnp.maximum(l0, l1), l2)
        w0, w1, w2 = jnp.exp(l0 - m), jnp.exp(l1 - m), jnp.exp(l2 - m)
        o = (w0 * o_sc[0, rows, :] + w1 * o_sc[1, rows, :] + w2 * o_sc[2, rows, :])
        o = o * (1.0 / (w0 + w1 + w2))
        gate = gate_ref[rows, :].astype(F32)
        y_ref[rows, :] = (o * _silu(gate)).astype(y_ref.dtype)
        return carry

    lax.fori_loop(0, seq // blk, combine, 0)


def dilated_attention(planes, bias, batch, seq):
    hpg = HEADS_PER_GROUP

    def plane_spec(offset):
        return pl.BlockSpec((None, seq, LANES), lambda bi, h: (offset + h, bi, 0))

    in_specs, args = [], []
    for g in range(N_GROUPS_ATTN):
        for part in range(3):
            in_specs.append(plane_spec(part * hpg))
            args.append(planes[g])
    in_specs.append(plane_spec(3 * hpg))
    args.append(planes[0])
    in_specs.append(pl.BlockSpec((N_GROUPS_ATTN, None, ATTN_BLOCK, 2 * ATTN_BLOCK),
                                 lambda bi, h: (0, h, 0, 0)))
    args.append(bias)
    return pl.pallas_call(
        _attn_kernel,
        grid=(batch, hpg),
        in_specs=in_specs,
        out_specs=pl.BlockSpec((seq, LANES), lambda bi, h: (bi, h)),
        out_shape=jax.ShapeDtypeStruct((batch * seq, D_ATTN), BF16),
        scratch_shapes=[pltpu.VMEM((N_GROUPS_ATTN, seq, LANES), F32),
                        pltpu.VMEM((N_GROUPS_ATTN, seq, LANES), F32)],
        compiler_params=_cparams("parallel", "parallel"),
        name="dilated_attention",
    )(*args)


LN_ROWS = 32


def _outproj_ln_kernel(y_ref, w_ref, x_ref, g_ref, b_ref, *o_refs, alpha, nk):
    o_ref = o_refs[0]
    k = pl.program_id(1)

    @pl.when(k == 0)
    def _():
        o_ref[...] = alpha * x_ref[...]

    o_ref[...] += jnp.dot(y_ref[...], w_ref[...], preferred_element_type=F32)

    @pl.when(k == nk - 1)
    def _():
        gamma = g_ref[...]
        beta = b_ref[...]

        def per_rows(i, carry):
            rows = pl.ds(pl.multiple_of(i * LN_ROWS, LN_ROWS), LN_ROWS)
            v = o_ref[rows, :]
            mu = jnp.mean(v, axis=-1, keepdims=True)
            c = v - mu
            var = jnp.mean(c * c, axis=-1, keepdims=True)
            out = c * lax.rsqrt(var + LN_EPS) * gamma + beta
            o_ref[rows, :] = out
            if len(o_refs) > 1:
                o_refs[1][rows, :] = out.astype(BF16)
            return carry

        lax.fori_loop(0, o_ref.shape[0] // LN_ROWS, per_rows, 0)


def outproj_layernorm(y, w, x_res, gamma, beta, alpha, bm, bk, emit_bf16):
    m, kdim = y.shape
    d_model = w.shape[1]
    nk = kdim // bk
    out_shape = [jax.ShapeDtypeStruct((m, d_model), F32)]
    out_specs = [pl.BlockSpec((bm, d_model), lambda i, k: (i, 0))]
    if emit_bf16:
        out_shape.append(jax.ShapeDtypeStruct((m, d_model), BF16))
        out_specs.append(pl.BlockSpec((bm, d_model), lambda i, k: (i, 0)))
    return pl.pallas_call(
        functools.partial(_outproj_ln_kernel, alpha=alpha, nk=nk),
        grid=(m // bm, nk),
        in_specs=[
            pl.BlockSpec((bm, bk), lambda i, k: (i, k)),
            pl.BlockSpec((bk, d_model), lambda i, k: (k, 0)),
            pl.BlockSpec((bm, d_model), lambda i, k: (i, 0)),
            pl.BlockSpec((1, d_model), lambda i, k: (0, 0)),
            pl.BlockSpec((1, d_model), lambda i, k: (0, 0)),
        ],
        out_specs=out_specs,
        out_shape=out_shape,
        compiler_params=_cparams("parallel", "arbitrary"),
        name="outproj_layernorm",
    )(y, w, x_res, gamma.reshape(1, d_model).astype(F32), beta.reshape(1, d_model).astype(F32))


def _dt_kernel(w_ref, x_ref, bias_ref, a_ref, dt_ref, da_ref):
    raw = lax.dot_general(w_ref[...], x_ref[...], (((1,), (1,)), ((), ())),
                          precision=lax.Precision.HIGHEST, preferred_element_type=F32)
    v = raw + bias_ref[...]
    dt = jnp.maximum(v, 0.0) + jnp.log1p(jnp.exp(-jnp.abs(v)))
    dt_ref[...] = dt
    da_ref[...] = dt * a_ref[...]


def dt_projection(x, w_dt_t, dt_bias, a_log, bm):
    m, d_model = x.shape
    nh = w_dt_t.shape[0]
    a_col = (-jnp.exp(a_log.astype(F32))).reshape(nh, 1)
    return pl.pallas_call(
        _dt_kernel,
        grid=(m // bm,),
        in_specs=[
            pl.BlockSpec((nh, d_model), lambda i: (0, 0)),
            pl.BlockSpec((bm, d_model), lambda i: (i, 0)),
            pl.BlockSpec((nh, 1), lambda i: (0, 0)),
            pl.BlockSpec((nh, 1), lambda i: (0, 0)),
        ],
        out_specs=[pl.BlockSpec((nh, bm), lambda i: (0, i)),
                   pl.BlockSpec((nh, bm), lambda i: (0, i))],
        out_shape=[jax.ShapeDtypeStruct((nh, m), F32), jax.ShapeDtypeStruct((nh, m), F32)],
        compiler_params=_cparams("parallel"),
        name="dt_projection",
    )(w_dt_t, x, dt_bias.astype(F32).reshape(nh, 1), a_col)


def _expand_pair(cols, c, lane_lo):
    shape = (cols.shape[0], LANES)
    lo = jnp.broadcast_to(cols[:, 2 * c:2 * c + 1], shape)
    hi = jnp.broadcast_to(cols[:, 2 * c + 1:2 * c + 2], shape)
    return jnp.where(lane_lo, lo, hi)


def _ssd_kernel(xs_ref, b_ref, c_ref, z_ref, dt_ref, da_ref,
                wx_ref, wb_ref, wc_ref, cbx_ref, cbb_ref, cbc_ref,
                dskip_ref, normw_ref, expand_ref, o_ref,
                state_sc, carry_x, carry_b, carry_c, *, nsub):
    L = CHUNK
    hp = lax.Precision.HIGHEST

    @pl.when(pl.program_id(2) == 0)
    def _():
        state_sc[...] = jnp.zeros_like(state_sc)
        carry_x[...] = jnp.zeros_like(carry_x)
        carry_b[...] = jnp.zeros_like(carry_b)
        carry_c[...] = jnp.zeros_like(carry_c)

    row_i = lax.broadcasted_iota(jnp.int32, (L, L), 0)
    col_i = lax.broadcasted_iota(jnp.int32, (L, L), 1)
    causal = row_i >= col_i
    triu = (row_i <= col_i).astype(F32)
    eye = (row_i == col_i).astype(F32)
    lane_lo = lax.broadcasted_iota(jnp.int32, (L, LANES), 1) < SSM_HEAD_DIM

    def conv_silu(cur, carry_ref, w_ref, bias_ref):
        win = jnp.concatenate([carry_ref[...], cur], axis=0)
        carry_ref[...] = cur[L - CARRY_ROWS:, :]
        acc = bias_ref[...]
        for k in range(CONV_WIDTH):
            off = CARRY_ROWS - (CONV_WIDTH - 1) + k
            acc = acc + w_ref[k:k + 1, :] * win[off:off + L, :]
        return _silu(acc)

    def per_chunk(j, carry):
        rows = pl.ds(pl.multiple_of(j * L, L), L)
        xs = conv_silu(xs_ref[rows, :].astype(F32), carry_x, wx_ref, cbx_ref)
        bm = conv_silu(b_ref[rows, :].astype(F32), carry_b, wb_ref, cbb_ref)
        cm = conv_silu(c_ref[rows, :].astype(F32), carry_c, wc_ref, cbc_ref)
        bm_b = bm.astype(BF16)

        dt_t = dt_ref[:, rows]
        acum_t = jnp.dot(da_ref[:, rows], triu, precision=hp,
                         preferred_element_type=F32)
        stacked = jnp.concatenate([dt_t, acum_t], axis=0)
        cols = lax.dot_general(eye, stacked, (((1,), (1,)), ((), ())), precision=hp,
                               preferred_element_type=F32)
        dt_cols = cols[:, :HEADS_PER_SSM_GROUP]
        acum_cols = cols[:, HEADS_PER_SSM_GROUP:]
        alast_row = acum_cols[L - 1:L, :]
        e_cols = jnp.exp(acum_cols)
        te_cols = jnp.exp(alast_row - acum_cols)
        decay_row = jnp.dot(jnp.broadcast_to(jnp.exp(alast_row), (8, HEADS_PER_SSM_GROUP)),
                            expand_ref[...], precision=hp,
                            preferred_element_type=F32)[0:1, :]

        cb = lax.dot_general(cm.astype(BF16), bm_b, (((1,), (1,)), ((), ())),
                             preferred_element_type=F32)

        state = state_sc[...]
        ys, xdt_te = [], []
        for c in range(PAIRS):
            lanes = slice(c * LANES, (c + 1) * LANES)
            xdt = xs[:, lanes] * _expand_pair(dt_cols, c, lane_lo)
            xdt_te.append((xdt * _expand_pair(te_cols, c, lane_lo)).astype(BF16))
            rhs = jnp.concatenate([xdt.astype(BF16), state[:, lanes].astype(BF16)], axis=0)
            pair = []
            for h in (2 * c, 2 * c + 1):
                seg = acum_cols[:, h:h + 1] - acum_t[h:h + 1, :]
                decay = jnp.exp(jnp.where(causal, seg, -jnp.inf))
                lhs = jnp.concatenate([(cb * decay).astype(BF16),
                                       (cm * e_cols[:, h:h + 1]).astype(BF16)], axis=1)
                pair.append(jnp.dot(lhs, rhs, preferred_element_type=F32))
            ys.append(jnp.where(lane_lo, pair[0], pair[1]))
        y = jnp.concatenate(ys, axis=1)

        upd = lax.dot_general(bm_b, jnp.concatenate(xdt_te, axis=1),
                              (((0,), (0,)), ((), ())), preferred_element_type=F32)
        state_sc[...] = state * decay_row + upd

        y = y + dskip_ref[...] * xs
        y = y * _silu(z_ref[rows, :].astype(F32))
        ms = jnp.mean(y * y, axis=-1, keepdims=True)
        y = y * lax.rsqrt(ms + RMS_EPS) * normw_ref[...]
        o_ref[rows, :] = y.astype(o_ref.dtype)
        return carry

    lax.fori_loop(0, nsub, per_chunk, 0)


def ssd_mixer_core(proj, dt_t, da_t, conv_w, conv_b, d_skip, norm_w, batch, seq, rows_per_step):
    m = proj.shape[0]
    d_inner = SSM_GROUPS * GROUP_CH
    t = rows_per_step
    nsteps = seq // t
    xcol0 = d_inner // GROUP_CH
    bcol0 = 2 * d_inner // D_STATE
    ccol0 = bcol0 + SSM_GROUPS
    wb0 = d_inner // D_STATE
    wc0 = wb0 + SSM_GROUPS
    conv_w = conv_w.astype(F32)
    conv_b = conv_b.astype(F32).reshape(1, -1)
    d_exp = jnp.repeat(d_skip.astype(F32), SSM_HEAD_DIM).reshape(1, d_inner)
    norm_w = norm_w.astype(F32).reshape(1, d_inner)
    expand = jnp.repeat(jnp.eye(HEADS_PER_SSM_GROUP, dtype=F32), SSM_HEAD_DIM, axis=1)

    def row(bi, s):
        return bi * nsteps + s

    in_specs = [
        pl.BlockSpec((t, GROUP_CH), lambda bi, g, s: (row(bi, s), xcol0 + g)),
        pl.BlockSpec((t, D_STATE), lambda bi, g, s: (row(bi, s), bcol0 + g)),
        pl.BlockSpec((t, D_STATE), lambda bi, g, s: (row(bi, s), ccol0 + g)),
        pl.BlockSpec((t, GROUP_CH), lambda bi, g, s: (row(bi, s), g)),
        pl.BlockSpec((HEADS_PER_SSM_GROUP, t), lambda bi, g, s: (g, row(bi, s))),
        pl.BlockSpec((HEADS_PER_SSM_GROUP, t), lambda bi, g, s: (g, row(bi, s))),
        pl.BlockSpec((CONV_WIDTH, GROUP_CH), lambda bi, g, s: (0, g)),
        pl.BlockSpec((CONV_WIDTH, D_STATE), lambda bi, g, s: (0, wb0 + g)),
        pl.BlockSpec((CONV_WIDTH, D_STATE), lambda bi, g, s: (0, wc0 + g)),
        pl.BlockSpec((1, GROUP_CH), lambda bi, g, s: (0, g)),
        pl.BlockSpec((1, D_STATE), lambda bi, g, s: (0, wb0 + g)),
        pl.BlockSpec((1, D_STATE), lambda bi, g, s: (0, wc0 + g)),
        pl.BlockSpec((1, GROUP_CH), lambda bi, g, s: (0, g)),
        pl.BlockSpec((1, GROUP_CH), lambda bi, g, s: (0, g)),
        pl.BlockSpec((HEADS_PER_SSM_GROUP, GROUP_CH), lambda bi, g, s: (0, 0)),
    ]
    return pl.pallas_call(
        functools.partial(_ssd_kernel, nsub=t // CHUNK),
        grid=(batch, SSM_GROUPS, nsteps),
        in_specs=in_specs,
        out_specs=pl.BlockSpec((t, GROUP_CH), lambda bi, g, s: (row(bi, s), g)),
        out_shape=jax.ShapeDtypeStruct((m, d_inner), BF16),
        scratch_shapes=[pltpu.VMEM((D_STATE, GROUP_CH), F32),
                        pltpu.VMEM((CARRY_ROWS, GROUP_CH), F32),
                        pltpu.VMEM((CARRY_ROWS, D_STATE), F32),
                        pltpu.VMEM((CARRY_ROWS, D_STATE), F32)],
        compiler_params=_cparams("parallel", "parallel", "arbitrary"),
        name="ssd_mixer_core",
    )(proj, proj, proj, proj, dt_t, da_t, conv_w, conv_w, conv_w, conv_b, conv_b, conv_b,
      d_exp, norm_w, expand)


def _attention_layer(x, xb, w_in, w_out, rel_bias, gamma, beta, alpha, emit_bf16):
    batch, seq, d_model = x.shape
    scale = HEAD_DIM ** -0.5
    group_cols = 3 * D_ATTN
    qkv_cols = N_GROUPS_ATTN * group_cols

    def group_weights(g, with_gate):
        c0 = g * group_cols
        parts = [w_in[:, c0:c0 + D_ATTN] * scale, w_in[:, c0 + D_ATTN:c0 + group_cols]]
        if with_gate:
            parts.append(w_in[:, qkv_cols:])
        return jnp.concatenate(parts, axis=1).astype(BF16)

    planes = [
        proj_planes(xb, group_weights(0, True), 1, bm=1024, bn=1024),
        proj_planes(xb, group_weights(1, False), 4, bm=1024, bn=1024),
        proj_planes(xb, group_weights(2, False), 16, bm=256, bn=2048),
    ]
    y = dilated_attention(planes, masked_bias(rel_bias), batch, seq)
    return outproj_layernorm(y, w_out.astype(BF16), x.reshape(batch * seq, d_model),
                             gamma, beta, alpha, bm=512, bk=512, emit_bf16=emit_bf16)


def _ssd_layer(x, xb, w_in, conv_w, conv_b, dt_bias, a_log, d_skip, norm_w, w_out,
               gamma, beta, alpha, emit_bf16):
    batch, seq, d_model = x.shape
    m = batch * seq
    nh = dt_bias.shape[0]
    n_main = w_in.shape[1] - nh
    xf = x.reshape(m, d_model)
    proj = proj_flat(xb.reshape(m, d_model), w_in[:, :n_main].astype(BF16), n_main,
                     bm=1024, bn=1024)
    dt_t, da_t = dt_projection(xf, w_in[:, n_main:].T.astype(F32), dt_bias, a_log, bm=512)
    y = ssd_mixer_core(proj, dt_t, da_t, conv_w, conv_b, d_skip, norm_w, batch, seq,
                       rows_per_step=512)
    return outproj_layernorm(y, w_out.astype(BF16), xf, gamma, beta, alpha,
                             bm=512, bk=512, emit_bf16=emit_bf16)


def kernel(x, w_in_attn, w_out_attn, rel_bias, w_in_ssm, conv_w, conv_b, dt_bias, a_log,
           d_skip, ssm_norm_w, w_out_ssm, ln_g, ln_b):
    batch, seq, d_model = x.shape
    depth = ln_g.shape[0]
    alpha = (2 * depth) ** 0.25
    xb = x.astype(BF16)
    for i in range(depth):
        j = i // 2
        last = i == depth - 1
        if i % 2 == 0:
            outs = _attention_layer(x, xb, w_in_attn[j], w_out_attn[j], rel_bias,
                                    ln_g[i], ln_b[i], alpha, not last)
        else:
            outs = _ssd_layer(x, xb, w_in_ssm[j], conv_w[j], conv_b[j], dt_bias[j], a_log[j],
                              d_skip[j], ssm_norm_w[j], w_out_ssm[j],
                              ln_g[i], ln_b[i], alpha, not last)
        x = outs[0].reshape(batch, seq, d_model)
        if not last:
            xb = outs[1].reshape(batch, seq, d_model)
    return x
```

```python
import functools
import math

import jax
import jax.numpy as jnp
from jax import lax
from jax.experimental import pallas as pl
from jax.experimental.pallas import tpu as pltpu

F32 = jnp.float32
BF16 = jnp.bfloat16

LANES = 128
VMEM_LIMIT_BYTES = 56 * 1024 * 1024

ATTN_PATTERNS = ((128, 1), (512, 4), (2048, 16))
N_GROUPS_ATTN = 3
HEADS_PER_GROUP = 16
HEAD_DIM = 128
D_ATTN = HEADS_PER_GROUP * HEAD_DIM
ATTN_BLOCK = 128
NUM_BUCKETS = 32
MAX_DISTANCE = 2048
SSM_HEAD_DIM = 64
SSM_GROUPS = 8
HEADS_PER_SSM_GROUP = 16
D_STATE = 128
CONV_WIDTH = 4
CHUNK = 128
LN_EPS = 1e-5
RMS_EPS = 1e-5
NEG_INF = -1e30

GROUP_CH = HEADS_PER_SSM_GROUP * SSM_HEAD_DIM
PAIRS = GROUP_CH // LANES
CARRY_ROWS = 8


def _cparams(*sem):
    return pltpu.CompilerParams(dimension_semantics=sem, vmem_limit_bytes=VMEM_LIMIT_BYTES)


def _cast_permute_kernel(x_ref, *o_refs, dilations):
    xb = x_ref[...].astype(BF16)
    bm = xb.shape[0]
    dst = lax.broadcasted_iota(jnp.int32, (bm, bm), 0)
    src = lax.broadcasted_iota(jnp.int32, (bm, bm), 1)
    for o_ref, d in zip(o_refs, dilations):
        rows = bm // d
        if d == 1:
            o_ref[0] = xb
            continue
        perm = (src == (dst % rows) * d + dst // rows).astype(BF16)
        xp = jnp.dot(perm, xb, preferred_element_type=F32).astype(BF16)
        for r in range(d):
            o_ref[r] = xp[r * rows:(r + 1) * rows, :]


def cast_permute(x, dilations, bm):
    b, s, d_model = x.shape
    return pl.pallas_call(
        functools.partial(_cast_permute_kernel, dilations=dilations),
        grid=(b, s // bm),
        in_specs=[pl.BlockSpec((None, bm, d_model), lambda bi, i: (bi, i, 0))],
        out_specs=[pl.BlockSpec((None, d, bm // d, d_model), lambda bi, i: (bi, 0, i, 0))
                   for d in dilations],
        out_shape=[jax.ShapeDtypeStruct((b, d, s // d, d_model), BF16) for d in dilations],
        compiler_params=_cparams("parallel", "parallel"),
        name="cast_permute",
    )(x)


def _proj_planes_kernel(x_ref, w_ref, o_ref, *, q_blocks, q_scale):
    acc = jnp.dot(x_ref[...], w_ref[...], preferred_element_type=F32)
    acc = acc * jnp.where(pl.program_id(0) < q_blocks, q_scale, 1.0)
    for c in range(o_ref.shape[0]):
        o_ref[c] = acc[:, c * LANES:(c + 1) * LANES].astype(o_ref.dtype)


def proj_planes(xp, w, col_block, n, q_cols, q_scale, bm, bn):
    b, dilation, sub_len, d_model = xp.shape
    bm = min(bm, sub_len)
    nbl = sub_len // bm
    grid = (n // bn, b, dilation, nbl)
    return pl.pallas_call(
        functools.partial(_proj_planes_kernel, q_blocks=q_cols // bn, q_scale=q_scale),
        grid=grid,
        in_specs=[
            pl.BlockSpec((None, None, bm, d_model), lambda j, bi, r, i: (bi, r, i, 0)),
            pl.BlockSpec((d_model, bn), lambda j, bi, r, i: (0, col_block(j))),
        ],
        out_specs=pl.BlockSpec(
            (bn // LANES, bm, LANES),
            lambda j, bi, r, i: (j, (bi * dilation + r) * nbl + i, 0)),
        out_shape=jax.ShapeDtypeStruct((n // LANES, b * dilation * sub_len, LANES), BF16),
        compiler_params=_cparams("parallel", "parallel", "parallel", "parallel"),
        name=f"proj_planes_d{dilation}",
    )(xp, w)


def _proj_flat_kernel(x_ref, w_ref, o_ref):
    o_ref[...] = jnp.dot(x_ref[...], w_ref[...],
                         preferred_element_type=F32).astype(o_ref.dtype)


def proj_flat(xb, w, n_out, bm, bn):
    m, d_model = xb.shape
    return pl.pallas_call(
        _proj_flat_kernel,
        grid=(n_out // bn, m // bm),
        in_specs=[
            pl.BlockSpec((bm, d_model), lambda j, i: (i, 0)),
            pl.BlockSpec((d_model, bn), lambda j, i: (0, j)),
        ],
        out_specs=pl.BlockSpec((bm, bn), lambda j, i: (i, j)),
        out_shape=jax.ShapeDtypeStruct((m, n_out), BF16),
        compiler_params=_cparams("parallel", "parallel"),
        name="proj_flat",
    )(xb, w)


def _t5_causal_bucket(dist):
    max_exact = NUM_BUCKETS // 2
    d_f = jnp.maximum(dist, 1).astype(F32)
    large = max_exact + (jnp.log(d_f / max_exact) / math.log(MAX_DISTANCE / max_exact)
                         * (NUM_BUCKETS - max_exact)).astype(jnp.int32)
    large = jnp.minimum(large, NUM_BUCKETS - 1)
    return jnp.where(dist < max_exact, dist, large)


def _band_buckets():
    qi = jnp.arange(ATTN_BLOCK)[:, None]
    ki = jnp.arange(2 * ATTN_BLOCK)[None, :]
    delta = ATTN_BLOCK + qi - ki
    out = []
    for window, dilation in ATTN_PATTERNS:
        span = window // dilation
        band = (delta >= 0) & (delta <= span)
        bucket = _t5_causal_bucket(jnp.clip(delta, 0, None) * dilation)
        out.append(jnp.where(band, bucket, -1))
    return jnp.stack(out).astype(jnp.int32)


def _bias_kernel(table_ref, bucket_ref, o_ref):
    g = pl.program_id(0)
    bucket = bucket_ref[...]
    prev_half = lax.broadcasted_iota(jnp.int32, bucket.shape, 1) < ATTN_BLOCK

    def per_head(h, carry):
        acc = jnp.full(bucket.shape, NEG_INF, F32)
        for b in range(NUM_BUCKETS):
            acc = jnp.where(bucket == b, table_ref[b, g * HEADS_PER_GROUP + h], acc)
        o_ref[h, 0] = jnp.where(prev_half, NEG_INF, acc)
        o_ref[h, 1] = acc
        return carry

    lax.fori_loop(0, HEADS_PER_GROUP, per_head, 0)


def masked_bias(rel_bias):
    buckets = _band_buckets()
    return pl.pallas_call(
        _bias_kernel,
        grid=(N_GROUPS_ATTN,),
        in_specs=[
            pl.BlockSpec(memory_space=pltpu.SMEM),
            pl.BlockSpec((None, ATTN_BLOCK, 2 * ATTN_BLOCK), lambda g: (g, 0, 0)),
        ],
        out_specs=pl.BlockSpec((None, HEADS_PER_GROUP, 2, ATTN_BLOCK, 2 * ATTN_BLOCK),
                               lambda g: (g, 0, 0, 0, 0)),
        out_shape=jax.ShapeDtypeStruct(
            (N_GROUPS_ATTN, HEADS_PER_GROUP, 2, ATTN_BLOCK, 2 * ATTN_BLOCK), F32),
        compiler_params=_cparams("parallel"),
        name="masked_bias",
    )(rel_bias.astype(F32), buckets)


def _silu(v):
    return v * (1.0 / (1.0 + jnp.exp(-v)))


ATTN_UNROLL = 8
ATTN_PROJ_TILES = ((1024, 1024), (1024, 1024), (256, 2048))


def _attn_kernel(q0, k0, v0, q1, k1, v1, q2, k2, v2, gate_ref, bias_ref, y_ref, o_sc, lse_sc):
    seq = y_ref.shape[0]
    blk = ATTN_BLOCK
    qkv = ((q0, k0, v0), (q1, k1, v1), (q2, k2, v2))

    def finish(g, dilation, tok0, s, vals):
        m = jnp.max(s, axis=-1, keepdims=True)
        p = jnp.exp(s - m)
        denom = jnp.sum(p, axis=-1, keepdims=True)
        acc = jnp.dot(p.astype(BF16), vals, preferred_element_type=F32)
        o = acc * (1.0 / denom)
        lse = jnp.broadcast_to(m + jnp.log(denom), (blk, LANES))
        if dilation == 1:
            rows = pl.ds(tok0, blk)
        else:
            rows = pl.ds(tok0, blk, stride=dilation)
        o_sc[g, rows, :] = o
        lse_sc[g, rows, :] = lse

    def scores(q, keys):
        return lax.dot_general(q, keys, (((1,), (1,)), ((), ())), preferred_element_type=F32)

    for g, (_, dilation) in enumerate(ATTN_PATTERNS):
        q_ref, k_ref, v_ref = qkv[g]
        nb = seq // dilation // blk

        def per_block(idx, carry, g=g, dilation=dilation, q_ref=q_ref, k_ref=k_ref,
                      v_ref=v_ref, nb=nb):
            r, n = idx // nb, idx % nb
            row0 = pl.multiple_of(idx * blk, blk)
            prev0 = pl.multiple_of(jnp.maximum(row0 - blk, 0), blk)
            keys = jnp.concatenate([k_ref[pl.ds(prev0, blk), :],
                                    k_ref[pl.ds(row0, blk), :]], axis=0)
            vals = jnp.concatenate([v_ref[pl.ds(prev0, blk), :],
                                    v_ref[pl.ds(row0, blk), :]], axis=0)
            s = scores(q_ref[pl.ds(row0, blk), :], keys) + bias_ref[g, jnp.minimum(n, 1)]
            finish(g, dilation, r + dilation * n * blk, s, vals)
            return carry

        lax.fori_loop(0, dilation * nb, per_block, 0, unroll=ATTN_UNROLL)

    def combine(i, carry):
        rows = pl.ds(pl.multiple_of(i * blk, blk), blk)
        l0, l1, l2 = lse_sc[0, rows, :], lse_sc[1, rows, :], lse_sc[2, rows, :]
        m = jnp.maximum(jnp.maximum(l0, l1), l2)
        w0, w1, w2 = jnp.exp(l0 - m), jnp.exp(l1 - m), jnp.exp(l2 - m)
        o = (w0 * o_sc[0, rows, :] + w1 * o_sc[1, rows, :] + w2 * o_sc[2, rows, :])
        o = o * (1.0 / (w0 + w1 + w2))
        gate = gate_ref[rows, :].astype(F32)
        y_ref[rows, :] = (o * _silu(gate)).astype(y_ref.dtype)
        return carry

    lax.fori_loop(0, seq // blk, combine, 0)


def dilated_attention(planes, bias, batch, seq):
    hpg = HEADS_PER_GROUP

    def plane_spec(offset):
        return pl.BlockSpec((None, seq, LANES), lambda bi, h: (offset + h, bi, 0))

    in_specs, args = [], []
    for g in range(N_GROUPS_ATTN):
        for part in range(3):
            in_specs.append(plane_spec(part * hpg))
            args.append(planes[g])
    in_specs.append(plane_spec(3 * hpg))
    args.append(planes[0])
    in_specs.append(pl.BlockSpec((N_GROUPS_ATTN, None, 2, ATTN_BLOCK, 2 * ATTN_BLOCK),
                                 lambda bi, h: (0, h, 0, 0, 0)))
    args.append(bias)
    return pl.pallas_call(
        _attn_kernel,
        grid=(batch, hpg),
        in_specs=in_specs,
        out_specs=pl.BlockSpec((seq, LANES), lambda bi, h: (bi, h)),
        out_shape=jax.ShapeDtypeStruct((batch * seq, D_ATTN), BF16),
        scratch_shapes=[pltpu.VMEM((N_GROUPS_ATTN, seq, LANES), F32),
                        pltpu.VMEM((N_GROUPS_ATTN, seq, LANES), F32)],
        compiler_params=_cparams("parallel", "parallel"),
        name="dilated_attention",
    )(*args)


LN_ROWS = 32


def _outproj_ln_kernel(y_ref, w_ref, x_ref, g_ref, b_ref, *o_refs, alpha, nk):
    o_ref = o_refs[0]
    k = pl.program_id(1)

    @pl.when(k == 0)
    def _():
        o_ref[...] = alpha * x_ref[...]

    o_ref[...] += jnp.dot(y_ref[...], w_ref[...], preferred_element_type=F32)

    @pl.when(k == nk - 1)
    def _():
        gamma = g_ref[...]
        beta = b_ref[...]

        def per_rows(i, carry):
            rows = pl.ds(pl.multiple_of(i * LN_ROWS, LN_ROWS), LN_ROWS)
            v = o_ref[rows, :]
            mu = jnp.mean(v, axis=-1, keepdims=True)
            c = v - mu
            var = jnp.mean(c * c, axis=-1, keepdims=True)
            out = c * lax.rsqrt(var + LN_EPS) * gamma + beta
            o_ref[rows, :] = out
            if len(o_refs) > 1:
                o_refs[1][rows, :] = out.astype(BF16)
            return carry

        lax.fori_loop(0, o_ref.shape[0] // LN_ROWS, per_rows, 0)


def outproj_layernorm(y, w, x_res, gamma, beta, alpha, bm, bk, emit_bf16):
    m, kdim = y.shape
    d_model = w.shape[1]
    nk = kdim // bk
    out_shape = [jax.ShapeDtypeStruct((m, d_model), F32)]
    out_specs = [pl.BlockSpec((bm, d_model), lambda i, k: (i, 0))]
    if emit_bf16:
        out_shape.append(jax.ShapeDtypeStruct((m, d_model), BF16))
        out_specs.append(pl.BlockSpec((bm, d_model), lambda i, k: (i, 0)))
    return pl.pallas_call(
        functools.partial(_outproj_ln_kernel, alpha=alpha, nk=nk),
        grid=(m // bm, nk),
        in_specs=[
            pl.BlockSpec((bm, bk), lambda i, k: (i, k)),
            pl.BlockSpec((bk, d_model), lambda i, k: (k, 0)),
            pl.BlockSpec((bm, d_model), lambda i, k: (i, 0)),
            pl.BlockSpec((1, d_model), lambda i, k: (0, 0)),
            pl.BlockSpec((1, d_model), lambda i, k: (0, 0)),
        ],
        out_specs=out_specs,
        out_shape=out_shape,
        compiler_params=_cparams("parallel", "arbitrary"),
        name="outproj_layernorm",
    )(y, w, x_res, gamma.reshape(1, d_model).astype(F32), beta.reshape(1, d_model).astype(F32))


def _dt_kernel(x_ref, w_ref, bias_ref, a_ref, dt_ref, da_ref):
    raw = jnp.dot(x_ref[...], w_ref[...], precision=lax.Precision.HIGHEST,
                  preferred_element_type=F32)
    v = raw + bias_ref[...]
    dt = jnp.maximum(v, 0.0) + jnp.log1p(jnp.exp(-jnp.abs(v)))
    dt_ref[...] = dt.T
    da_ref[...] = (dt * a_ref[...]).T


def dt_projection(x, w_in, dt_bias, a_log, bm):
    m, d_model = x.shape
    nh = dt_bias.shape[0]
    dt_block = w_in.shape[1] // nh - 1
    a_row = (-jnp.exp(a_log.astype(F32))).reshape(1, nh)
    return pl.pallas_call(
        _dt_kernel,
        grid=(m // bm,),
        in_specs=[
            pl.BlockSpec((bm, d_model), lambda i: (i, 0)),
            pl.BlockSpec((d_model, nh), lambda i: (0, dt_block)),
            pl.BlockSpec((1, nh), lambda i: (0, 0)),
            pl.BlockSpec((1, nh), lambda i: (0, 0)),
        ],
        out_specs=[pl.BlockSpec((nh, bm), lambda i: (0, i)),
                   pl.BlockSpec((nh, bm), lambda i: (0, i))],
        out_shape=[jax.ShapeDtypeStruct((nh, m), F32), jax.ShapeDtypeStruct((nh, m), F32)],
        compiler_params=_cparams("parallel"),
        name="dt_projection",
    )(x, w_in, dt_bias.astype(F32).reshape(1, nh), a_row)


def _expand_pair(cols, c, lane_lo):
    shape = (cols.shape[0], LANES)
    lo = jnp.broadcast_to(cols[:, 2 * c:2 * c + 1], shape)
    hi = jnp.broadcast_to(cols[:, 2 * c + 1:2 * c + 2], shape)
    return jnp.where(lane_lo, lo, hi)


def _ssd_kernel(xs_ref, b_ref, c_ref, z_ref, dt_ref, da_ref,
                wx_ref, wb_ref, wc_ref, cbx_ref, cbb_ref, cbc_ref,
                dskip_ref, normw_ref, expand_ref, o_ref,
                state_sc, carry_x, carry_b, carry_c, *, nsub):
    L = CHUNK
    hp = lax.Precision.HIGHEST

    @pl.when(pl.program_id(2) == 0)
    def _():
        state_sc[...] = jnp.zeros_like(state_sc)
        carry_x[...] = jnp.zeros_like(carry_x)
        carry_b[...] = jnp.zeros_like(carry_b)
        carry_c[...] = jnp.zeros_like(carry_c)

    row_i = lax.broadcasted_iota(jnp.int32, (L, L), 0)
    col_i = lax.broadcasted_iota(jnp.int32, (L, L), 1)
    causal = row_i >= col_i
    triu = (row_i <= col_i).astype(F32)
    eye = (row_i == col_i).astype(F32)
    lane_lo = lax.broadcasted_iota(jnp.int32, (L, LANES), 1) < SSM_HEAD_DIM

    def conv_silu(cur, carry_ref, w_ref, bias_ref):
        tail = carry_ref[...]
        carry_ref[...] = cur[L - CARRY_ROWS:, :]
        row = lax.broadcasted_iota(jnp.int32, tail.shape, 0)
        acc = bias_ref[...] + w_ref[CONV_WIDTH - 1:CONV_WIDTH, :] * cur
        for s in range(1, CONV_WIDTH):
            rolled = pltpu.roll(cur, s, 0)
            head = jnp.where(row < s, pltpu.roll(tail, s, 0), rolled[:CARRY_ROWS])
            shifted = jnp.concatenate([head, rolled[CARRY_ROWS:]], axis=0)
            k = CONV_WIDTH - 1 - s
            acc = acc + w_ref[k:k + 1, :] * shifted
        return _silu(acc)

    def per_chunk(j, carry):
        rows = pl.ds(pl.multiple_of(j * L, L), L)
        xs = conv_silu(xs_ref[rows, :].astype(F32), carry_x, wx_ref, cbx_ref)
        bm = conv_silu(b_ref[rows, :].astype(F32), carry_b, wb_ref, cbb_ref)
        cm = conv_silu(c_ref[rows, :].astype(F32), carry_c, wc_ref, cbc_ref)
        bm_b = bm.astype(BF16)

        dt_t = dt_ref[:, rows]
        acum_t = jnp.dot(da_ref[:, rows], triu, precision=hp,
                         preferred_element_type=F32)
        stacked = jnp.concatenate([dt_t, acum_t], axis=0)
        cols = lax.dot_general(eye, stacked, (((1,), (1,)), ((), ())), precision=hp,
                               preferred_element_type=F32)
        dt_cols = cols[:, :HEADS_PER_SSM_GROUP]
        acum_cols = cols[:, HEADS_PER_SSM_GROUP:]
        alast_row = acum_cols[L - 1:L, :]
        e_cols = jnp.exp(acum_cols)
        te_cols = jnp.exp(alast_row - acum_cols)
        decay_row = jnp.dot(jnp.broadcast_to(jnp.exp(alast_row), (8, HEADS_PER_SSM_GROUP)),
                            expand_ref[...], precision=hp,
                            preferred_element_type=F32)[0:1, :]

        cb = lax.dot_general(cm.astype(BF16), bm_b, (((1,), (1,)), ((), ())),
                             preferred_element_type=F32)

        state = state_sc[...]
        ys, xdt_te = [], []
        for c in range(PAIRS):
            lanes = slice(c * LANES, (c + 1) * LANES)
            xdt = xs[:, lanes] * _expand_pair(dt_cols, c, lane_lo)
            xdt_te.append((xdt * _expand_pair(te_cols, c, lane_lo)).astype(BF16))
            rhs = jnp.concatenate([xdt.astype(BF16), state[:, lanes].astype(BF16)], axis=0)
            pair = []
            for h in (2 * c, 2 * c + 1):
                seg = acum_cols[:, h:h + 1] - acum_t[h:h + 1, :]
                decay = jnp.exp(jnp.where(causal, seg, -jnp.inf))
                lhs = jnp.concatenate([(cb * decay).astype(BF16),
                                       (cm * e_cols[:, h:h + 1]).astype(BF16)], axis=1)
                pair.append(jnp.dot(lhs, rhs, preferred_element_type=F32))
            ys.append(jnp.where(lane_lo, pair[0], pair[1]))
        y = jnp.concatenate(ys, axis=1)

        upd = lax.dot_general(bm_b, jnp.concatenate(xdt_te, axis=1),
                              (((0,), (0,)), ((), ())), preferred_element_type=F32)
        state_sc[...] = state * decay_row + upd

        y = y + dskip_ref[...] * xs
        y = y * _silu(z_ref[rows, :].astype(F32))
        ms = jnp.mean(y * y, axis=-1, keepdims=True)
        y = y * lax.rsqrt(ms + RMS_EPS) * normw_ref[...]
        o_ref[rows, :] = y.astype(o_ref.dtype)
        return carry

    lax.fori_loop(0, nsub, per_chunk, 0)


def ssd_mixer_core(proj, dt_t, da_t, conv_w, conv_b, d_skip, norm_w, batch, seq, rows_per_step):
    m = proj.shape[0]
    d_inner = SSM_GROUPS * GROUP_CH
    t = rows_per_step
    nsteps = seq // t
    xcol0 = d_inner // GROUP_CH
    bcol0 = 2 * d_inner // D_STATE
    ccol0 = bcol0 + SSM_GROUPS
    wb0 = d_inner // D_STATE
    wc0 = wb0 + SSM_GROUPS
    conv_w = conv_w.astype(F32)
    conv_b = conv_b.astype(F32).reshape(1, -1)
    d_exp = jnp.repeat(d_skip.astype(F32), SSM_HEAD_DIM).reshape(1, d_inner)
    norm_w = norm_w.astype(F32).reshape(1, d_inner)
    expand = jnp.repeat(jnp.eye(HEADS_PER_SSM_GROUP, dtype=F32), SSM_HEAD_DIM, axis=1)

    def row(bi, s):
        return bi * nsteps + s

    in_specs = [
        pl.BlockSpec((t, GROUP_CH), lambda bi, g, s: (row(bi, s), xcol0 + g)),
        pl.BlockSpec((t, D_STATE), lambda bi, g, s: (row(bi, s), bcol0 + g)),
        pl.BlockSpec((t, D_STATE), lambda bi, g, s: (row(bi, s), ccol0 + g)),
        pl.BlockSpec((t, GROUP_CH), lambda bi, g, s: (row(bi, s), g)),
        pl.BlockSpec((HEADS_PER_SSM_GROUP, t), lambda bi, g, s: (g, row(bi, s))),
        pl.BlockSpec((HEADS_PER_SSM_GROUP, t), lambda bi, g, s: (g, row(bi, s))),
        pl.BlockSpec((CONV_WIDTH, GROUP_CH), lambda bi, g, s: (0, g)),
        pl.BlockSpec((CONV_WIDTH, D_STATE), lambda bi, g, s: (0, wb0 + g)),
        pl.BlockSpec((CONV_WIDTH, D_STATE), lambda bi, g, s: (0, wc0 + g)),
        pl.BlockSpec((1, GROUP_CH), lambda bi, g, s: (0, g)),
        pl.BlockSpec((1, D_STATE), lambda bi, g, s: (0, wb0 + g)),
        pl.BlockSpec((1, D_STATE), lambda bi, g, s: (0, wc0 + g)),
        pl.BlockSpec((1, GROUP_CH), lambda bi, g, s: (0, g)),
        pl.BlockSpec((1, GROUP_CH), lambda bi, g, s: (0, g)),
        pl.BlockSpec((HEADS_PER_SSM_GROUP, GROUP_CH), lambda bi, g, s: (0, 0)),
    ]
    return pl.pallas_call(
        functools.partial(_ssd_kernel, nsub=t // CHUNK),
        grid=(batch, SSM_GROUPS, nsteps),
        in_specs=in_specs,
        out_specs=pl.BlockSpec((t, GROUP_CH), lambda bi, g, s: (row(bi, s), g)),
        out_shape=jax.ShapeDtypeStruct((m, d_inner), BF16),
        scratch_shapes=[pltpu.VMEM((D_STATE, GROUP_CH), F32),
                        pltpu.VMEM((CARRY_ROWS, GROUP_CH), F32),
                        pltpu.VMEM((CARRY_ROWS, D_STATE), F32),
                        pltpu.VMEM((CARRY_ROWS, D_STATE), F32)],
        compiler_params=_cparams("parallel", "parallel", "arbitrary"),
        name="ssd_mixer_core",
    )(proj, proj, proj, proj, dt_t, da_t, conv_w, conv_w, conv_w, conv_b, conv_b, conv_b,
      d_exp, norm_w, expand)


def _attention_layer(x, w_in, w_out, rel_bias, gamma, beta, alpha, emit_bf16):
    batch, seq, d_model = x.shape
    dilations = tuple(d for _, d in ATTN_PATTERNS)
    group_cols = 3 * D_ATTN
    gate_col0 = N_GROUPS_ATTN * group_cols
    wb = w_in.astype(BF16)
    xperm = cast_permute(x, dilations, bm=256)
    planes = []
    for g, (bm, bn) in enumerate(ATTN_PROJ_TILES):
        first = g * group_cols // bn
        if g == 0:
            nqkv, gate_first = group_cols // bn, gate_col0 // bn
            n = group_cols + D_ATTN
            col_block = lambda j, nqkv=nqkv, gate_first=gate_first: jnp.where(
                j < nqkv, j, j - nqkv + gate_first)
        else:
            n = group_cols
            col_block = lambda j, first=first: first + j
        planes.append(proj_planes(xperm[g], wb, col_block, n, D_ATTN, HEAD_DIM ** -0.5, bm, bn))
    y = dilated_attention(planes, masked_bias(rel_bias), batch, seq)
    return outproj_layernorm(y, w_out.astype(BF16), x.reshape(batch * seq, d_model),
                             gamma, beta, alpha, bm=512, bk=512, emit_bf16=emit_bf16)


def _ssd_layer(x, xb, w_in, conv_w, conv_b, dt_bias, a_log, d_skip, norm_w, w_out,
               gamma, beta, alpha):
    batch, seq, d_model = x.shape
    m = batch * seq
    n_main = w_in.shape[1] - dt_bias.shape[0]
    xf = x.reshape(m, d_model)
    proj = proj_flat(xb.reshape(m, d_model), w_in.astype(BF16), n_main, bm=1024, bn=1024)
    dt_t, da_t = dt_projection(xf, w_in, dt_bias, a_log, bm=512)
    y = ssd_mixer_core(proj, dt_t, da_t, conv_w, conv_b, d_skip, norm_w, batch, seq,
                       rows_per_step=512)
    return outproj_layernorm(y, w_out.astype(BF16), xf, gamma, beta, alpha,
                             bm=512, bk=512, emit_bf16=False)


def kernel(x, w_in_attn, w_out_attn, rel_bias, w_in_ssm, conv_w, conv_b, dt_bias, a_log,
           d_skip, ssm_norm_w, w_out_ssm, ln_g, ln_b):
    batch, seq, d_model = x.shape
    depth = ln_g.shape[0]
    alpha = (2 * depth) ** 0.25
    xb = None
    for i in range(depth):
        j = i // 2
        if i % 2 == 0:
            outs = _attention_layer(x, w_in_attn[j], w_out_attn[j], rel_bias,
                                    ln_g[i], ln_b[i], alpha, emit_bf16=i + 1 < depth)
            if i + 1 < depth:
                xb = outs[1].reshape(batch, seq, d_model)
        else:
            outs = _ssd_layer(x, xb, w_in_ssm[j], conv_w[j], conv_b[j], dt_bias[j], a_log[j],
                              d_skip[j], ssm_norm_w[j], w_out_ssm[j], ln_g[i], ln_b[i], alpha)
        x = outs[0].reshape(batch, seq, d_model)
    return x
```

```python
import functools
import math

import jax
import jax.numpy as jnp
from jax import lax
from jax.experimental import pallas as pl
from jax.experimental.pallas import tpu as pltpu

F32 = jnp.float32
BF16 = jnp.bfloat16

LANES = 128
VMEM_LIMIT_BYTES = 56 * 1024 * 1024

ATTN_PATTERNS = ((128, 1), (512, 4), (2048, 16))
N_GROUPS_ATTN = 3
HEADS_PER_GROUP = 16
HEAD_DIM = 128
D_ATTN = HEADS_PER_GROUP * HEAD_DIM
ATTN_BLOCK = 128
NUM_BUCKETS = 32
MAX_DISTANCE = 2048
SSM_HEAD_DIM = 64
SSM_GROUPS = 8
HEADS_PER_SSM_GROUP = 16
D_STATE = 128
CONV_WIDTH = 4
CHUNK = 128
LN_EPS = 1e-5
RMS_EPS = 1e-5
NEG_INF = -1e30

GROUP_CH = HEADS_PER_SSM_GROUP * SSM_HEAD_DIM
PAIRS = GROUP_CH // LANES
CONV_TAIL_ROWS = 8


def _cparams(*sem):
    return pltpu.CompilerParams(dimension_semantics=sem, vmem_limit_bytes=VMEM_LIMIT_BYTES)


def _cast_permute_kernel(x_ref, *o_refs, dilations):
    xb = x_ref[...].astype(BF16)
    bm = xb.shape[0]
    dst = lax.broadcasted_iota(jnp.int32, (bm, bm), 0)
    src = lax.broadcasted_iota(jnp.int32, (bm, bm), 1)
    for o_ref, d in zip(o_refs, dilations):
        rows = bm // d
        if d == 1:
            o_ref[0] = xb
            continue
        perm = (src == (dst % rows) * d + dst // rows).astype(BF16)
        xp = jnp.dot(perm, xb, preferred_element_type=F32).astype(BF16)
        for r in range(d):
            o_ref[r] = xp[r * rows:(r + 1) * rows, :]


def cast_permute(x, dilations, bm):
    b, s, d_model = x.shape
    return pl.pallas_call(
        functools.partial(_cast_permute_kernel, dilations=dilations),
        grid=(b, s // bm),
        in_specs=[pl.BlockSpec((None, bm, d_model), lambda bi, i: (bi, i, 0))],
        out_specs=[pl.BlockSpec((None, d, bm // d, d_model), lambda bi, i: (bi, 0, i, 0))
                   for d in dilations],
        out_shape=[jax.ShapeDtypeStruct((b, d, s // d, d_model), BF16) for d in dilations],
        compiler_params=_cparams("parallel", "parallel"),
        name="cast_permute",
    )(x)


def _proj_planes_kernel(x_ref, w_ref, o_ref, *, q_blocks, q_scale):
    acc = jnp.dot(x_ref[...], w_ref[...], preferred_element_type=F32)
    acc = acc * jnp.where(pl.program_id(0) < q_blocks, q_scale, 1.0)
    for c in range(o_ref.shape[0]):
        o_ref[c] = acc[:, c * LANES:(c + 1) * LANES].astype(o_ref.dtype)


def proj_planes(xp, w, col_block, n, q_cols, q_scale, bm, bn):
    b, dilation, sub_len, d_model = xp.shape
    bm = min(bm, sub_len)
    nbl = sub_len // bm
    grid = (n // bn, b, dilation, nbl)
    return pl.pallas_call(
        functools.partial(_proj_planes_kernel, q_blocks=q_cols // bn, q_scale=q_scale),
        grid=grid,
        in_specs=[
            pl.BlockSpec((None, None, bm, d_model), lambda j, bi, r, i: (bi, r, i, 0)),
            pl.BlockSpec((d_model, bn), lambda j, bi, r, i: (0, col_block(j))),
        ],
        out_specs=pl.BlockSpec(
            (bn // LANES, bm, LANES),
            lambda j, bi, r, i: (j, (bi * dilation + r) * nbl + i, 0)),
        out_shape=jax.ShapeDtypeStruct((n // LANES, b * dilation * sub_len, LANES), BF16),
        compiler_params=_cparams("parallel", "parallel", "parallel", "parallel"),
        name=f"proj_planes_d{dilation}",
    )(xp, w)


def _proj_flat_kernel(x_ref, w_ref, o_ref):
    o_ref[...] = jnp.dot(x_ref[...], w_ref[...],
                         preferred_element_type=F32).astype(o_ref.dtype)


def proj_flat(xb, w, n_out, bm, bn):
    m, d_model = xb.shape
    return pl.pallas_call(
        _proj_flat_kernel,
        grid=(n_out // bn, m // bm),
        in_specs=[
            pl.BlockSpec((bm, d_model), lambda j, i: (i, 0)),
            pl.BlockSpec((d_model, bn), lambda j, i: (0, j)),
        ],
        out_specs=pl.BlockSpec((bm, bn), lambda j, i: (i, j)),
        out_shape=jax.ShapeDtypeStruct((m, n_out), BF16),
        compiler_params=_cparams("parallel", "parallel"),
        name="proj_flat",
    )(xb, w)


def _t5_causal_bucket(dist):
    max_exact = NUM_BUCKETS // 2
    d_f = jnp.maximum(dist, 1).astype(F32)
    large = max_exact + (jnp.log(d_f / max_exact) / math.log(MAX_DISTANCE / max_exact)
                         * (NUM_BUCKETS - max_exact)).astype(jnp.int32)
    large = jnp.minimum(large, NUM_BUCKETS - 1)
    return jnp.where(dist < max_exact, dist, large)


def _band_buckets():
    qi = jnp.arange(ATTN_BLOCK)[:, None]
    ki = jnp.arange(2 * ATTN_BLOCK)[None, :]
    delta = ATTN_BLOCK + qi - ki
    out = []
    for window, dilation in ATTN_PATTERNS:
        span = window // dilation
        band = (delta >= 0) & (delta <= span)
        bucket = _t5_causal_bucket(jnp.clip(delta, 0, None) * dilation)
        out.append(jnp.where(band, bucket, -1))
    return jnp.stack(out).astype(jnp.int32)


def _bias_kernel(table_ref, bucket_ref, o_ref):
    g = pl.program_id(0)
    bucket = bucket_ref[...]
    prev_half = lax.broadcasted_iota(jnp.int32, bucket.shape, 1) < ATTN_BLOCK

    def per_head(h, carry):
        acc = jnp.full(bucket.shape, NEG_INF, F32)
        for b in range(NUM_BUCKETS):
            acc = jnp.where(bucket == b, table_ref[b, g * HEADS_PER_GROUP + h], acc)
        o_ref[h, 0] = jnp.where(prev_half, NEG_INF, acc)
        o_ref[h, 1] = acc
        return carry

    lax.fori_loop(0, HEADS_PER_GROUP, per_head, 0)


def masked_bias(rel_bias):
    buckets = _band_buckets()
    return pl.pallas_call(
        _bias_kernel,
        grid=(N_GROUPS_ATTN,),
        in_specs=[
            pl.BlockSpec(memory_space=pltpu.SMEM),
            pl.BlockSpec((None, ATTN_BLOCK, 2 * ATTN_BLOCK), lambda g: (g, 0, 0)),
        ],
        out_specs=pl.BlockSpec((None, HEADS_PER_GROUP, 2, ATTN_BLOCK, 2 * ATTN_BLOCK),
                               lambda g: (g, 0, 0, 0, 0)),
        out_shape=jax.ShapeDtypeStruct(
            (N_GROUPS_ATTN, HEADS_PER_GROUP, 2, ATTN_BLOCK, 2 * ATTN_BLOCK), F32),
        compiler_params=_cparams("parallel"),
        name="masked_bias",
    )(rel_bias.astype(F32), buckets)


def _silu(v):
    return v * (1.0 / (1.0 + jnp.exp(-v)))


ATTN_UNROLL = 8
ATTN_PROJ_TILES = ((1024, 1024), (1024, 1024), (256, 2048))


def _attn_kernel(q0, k0, v0, q1, k1, v1, q2, k2, v2, gate_ref, bias_ref, y_ref, o_sc, lse_sc):
    seq = y_ref.shape[0]
    blk = ATTN_BLOCK
    qkv = ((q0, k0, v0), (q1, k1, v1), (q2, k2, v2))

    def finish(g, dilation, tok0, s, vals):
        m = jnp.max(s, axis=-1, keepdims=True)
        p = jnp.exp(s - m)
        denom = jnp.sum(p, axis=-1, keepdims=True)
        acc = jnp.dot(p.astype(BF16), vals, preferred_element_type=F32)
        o = acc * (1.0 / denom)
        lse = jnp.broadcast_to(m + jnp.log(denom), (blk, LANES))
        if dilation == 1:
            rows = pl.ds(tok0, blk)
        else:
            rows = pl.ds(tok0, blk, stride=dilation)
        o_sc[g, rows, :] = o
        lse_sc[g, rows, :] = lse

    def scores(q, keys):
        return lax.dot_general(q, keys, (((1,), (1,)), ((), ())), preferred_element_type=F32)

    for g, (_, dilation) in enumerate(ATTN_PATTERNS):
        q_ref, k_ref, v_ref = qkv[g]
        nb = seq // dilation // blk

        def per_block(idx, carry, g=g, dilation=dilation, q_ref=q_ref, k_ref=k_ref,
                      v_ref=v_ref, nb=nb):
            r, n = idx // nb, idx % nb
            row0 = pl.multiple_of(idx * blk, blk)
            prev0 = pl.multiple_of(jnp.maximum(row0 - blk, 0), blk)
            keys = jnp.concatenate([k_ref[pl.ds(prev0, blk), :],
                                    k_ref[pl.ds(row0, blk), :]], axis=0)
            vals = jnp.concatenate([v_ref[pl.ds(prev0, blk), :],
                                    v_ref[pl.ds(row0, blk), :]], axis=0)
            s = scores(q_ref[pl.ds(row0, blk), :], keys) + bias_ref[g, jnp.minimum(n, 1)]
            finish(g, dilation, r + dilation * n * blk, s, vals)
            return carry

        lax.fori_loop(0, dilation * nb, per_block, 0, unroll=ATTN_UNROLL)

    def combine(i, carry):
        rows = pl.ds(pl.multiple_of(i * blk, blk), blk)
        l0, l1, l2 = lse_sc[0, rows, :], lse_sc[1, rows, :], lse_sc[2, rows, :]
        m = jnp.maximum(jnp.maximum(l0, l1), l2)
        w0, w1, w2 = jnp.exp(l0 - m), jnp.exp(l1 - m), jnp.exp(l2 - m)
        o = (w0 * o_sc[0, rows, :] + w1 * o_sc[1, rows, :] + w2 * o_sc[2, rows, :])
        o = o * (1.0 / (w0 + w1 + w2))
        gate = gate_ref[rows, :].astype(F32)
        y_ref[rows, :] = (o * _silu(gate)).astype(y_ref.dtype)
        return carry

    lax.fori_loop(0, seq // blk, combine, 0)


def dilated_attention(planes, bias, batch, seq):
    hpg = HEADS_PER_GROUP

    def plane_spec(offset):
        return pl.BlockSpec((None, seq, LANES), lambda bi, h: (offset + h, bi, 0))

    in_specs, args = [], []
    for g in range(N_GROUPS_ATTN):
        for part in range(3):
            in_specs.append(plane_spec(part * hpg))
            args.append(planes[g])
    in_specs.append(plane_spec(3 * hpg))
    args.append(planes[0])
    in_specs.append(pl.BlockSpec((N_GROUPS_ATTN, None, 2, ATTN_BLOCK, 2 * ATTN_BLOCK),
                                 lambda bi, h: (0, h, 0, 0, 0)))
    args.append(bias)
    return pl.pallas_call(
        _attn_kernel,
        grid=(batch, hpg),
        in_specs=in_specs,
        out_specs=pl.BlockSpec((seq, LANES), lambda bi, h: (bi, h)),
        out_shape=jax.ShapeDtypeStruct((batch * seq, D_ATTN), BF16),
        scratch_shapes=[pltpu.VMEM((N_GROUPS_ATTN, seq, LANES), F32),
                        pltpu.VMEM((N_GROUPS_ATTN, seq, LANES), F32)],
        compiler_params=_cparams("parallel", "parallel"),
        name="dilated_attention",
    )(*args)


LN_ROWS = 32


def _outproj_ln_kernel(y_ref, w_ref, x_ref, g_ref, b_ref, *refs, alpha, nt, nk):
    o_refs, accs = refs[:-2], refs[-2:]
    i, k = pl.program_id(0), pl.program_id(1)
    rc = x_ref.shape[0]
    d_model = w_ref.shape[1]
    n_ln = rc // LN_ROWS
    n_mm = n_ln if d_model % (n_ln * 2 * LANES) == 0 else 1
    cols = d_model // n_mm

    def matmul_part(acc_ref, first, c):
        sl = slice(c * cols, (c + 1) * cols)
        d = jnp.dot(y_ref[...], w_ref[:, sl], preferred_element_type=F32)
        if first:
            acc_ref[:, sl] = d
        else:
            acc_ref[:, sl] += d

    def layer_norm_part(done_ref, j):
        sub = pl.ds(j * LN_ROWS, LN_ROWS)
        src = pl.ds(pl.multiple_of(k * rc + j * LN_ROWS, LN_ROWS), LN_ROWS)
        v = done_ref[src, :] + alpha * x_ref[sub, :]
        mu = jnp.mean(v, axis=-1, keepdims=True)
        c = v - mu
        var = jnp.mean(c * c, axis=-1, keepdims=True)
        out = c * lax.rsqrt(var + LN_EPS) * g_ref[...] + b_ref[...]
        o_refs[0][sub, :] = out
        if len(o_refs) > 1:
            o_refs[1][sub, :] = out.astype(BF16)

    for parity in (0, 1):
        acc_ref, done_ref = accs[parity], accs[1 - parity]
        mine = (i % 2 == parity)
        for first in (True, False):
            cond = mine & ((k == 0) if first else (k > 0))

            if parity == 0:
                @pl.when(cond & (i == 0))
                def _(acc_ref=acc_ref, first=first):
                    for c in range(n_mm):
                        matmul_part(acc_ref, first, c)

            @pl.when(cond & (i > 0) & (i < nt))
            def _(acc_ref=acc_ref, done_ref=done_ref, first=first):
                for step in range(max(n_ln, n_mm)):
                    if step < n_mm:
                        matmul_part(acc_ref, first, step)
                    if step < n_ln:
                        layer_norm_part(done_ref, step)

        if parity == nt % 2:
            @pl.when(i == nt)
            def _(done_ref=done_ref):
                for j in range(n_ln):
                    layer_norm_part(done_ref, j)


def outproj_layernorm(y, w, x_res, gamma, beta, alpha, bm, bk, emit_bf16):
    m, kdim = y.shape
    d_model = w.shape[1]
    nt, nk = m // bm, kdim // bk
    assert nk >= 2
    rc = bm // nk

    def chunk(i, k):
        return (jnp.where(i == 0, 0, (i - 1) * nk + k), 0)

    out_shape = [jax.ShapeDtypeStruct((m, d_model), F32)]
    out_specs = [pl.BlockSpec((rc, d_model), chunk)]
    if emit_bf16:
        out_shape.append(jax.ShapeDtypeStruct((m, d_model), BF16))
        out_specs.append(pl.BlockSpec((rc, d_model), chunk))
    return pl.pallas_call(
        functools.partial(_outproj_ln_kernel, alpha=alpha, nt=nt, nk=nk),
        grid=(nt + 1, nk),
        in_specs=[
            pl.BlockSpec((bm, bk), lambda i, k: (jnp.minimum(i, nt - 1), k)),
            pl.BlockSpec((bk, d_model), lambda i, k: (jnp.where(i == nt, 0, k), 0)),
            pl.BlockSpec((rc, d_model), chunk),
            pl.BlockSpec((1, d_model), lambda i, k: (0, 0)),
            pl.BlockSpec((1, d_model), lambda i, k: (0, 0)),
        ],
        out_specs=out_specs,
        out_shape=out_shape,
        scratch_shapes=[pltpu.VMEM((bm, d_model), F32), pltpu.VMEM((bm, d_model), F32)],
        compiler_params=_cparams("arbitrary", "arbitrary"),
        name="outproj_layernorm",
    )(y, w, x_res, gamma.reshape(1, d_model).astype(F32), beta.reshape(1, d_model).astype(F32))


def _dt_kernel(x_ref, w_ref, bias_ref, a_ref, dt_ref, da_ref):
    raw = jnp.dot(x_ref[...], w_ref[...], precision=lax.Precision.HIGHEST,
                  preferred_element_type=F32)
    v = raw + bias_ref[...]
    dt = jnp.maximum(v, 0.0) + jnp.log1p(jnp.exp(-jnp.abs(v)))
    dt_ref[...] = dt.T
    da_ref[...] = (dt * a_ref[...]).T


def dt_projection(x, w_in, dt_bias, a_log, bm):
    m, d_model = x.shape
    nh = dt_bias.shape[0]
    dt_block = w_in.shape[1] // nh - 1
    a_row = (-jnp.exp(a_log.astype(F32))).reshape(1, nh)
    return pl.pallas_call(
        _dt_kernel,
        grid=(m // bm,),
        in_specs=[
            pl.BlockSpec((bm, d_model), lambda i: (i, 0)),
            pl.BlockSpec((d_model, nh), lambda i: (0, dt_block)),
            pl.BlockSpec((1, nh), lambda i: (0, 0)),
            pl.BlockSpec((1, nh), lambda i: (0, 0)),
        ],
        out_specs=[pl.BlockSpec((nh, bm), lambda i: (0, i)),
                   pl.BlockSpec((nh, bm), lambda i: (0, i))],
        out_shape=[jax.ShapeDtypeStruct((nh, m), F32), jax.ShapeDtypeStruct((nh, m), F32)],
        compiler_params=_cparams("parallel"),
        name="dt_projection",
    )(x, w_in, dt_bias.astype(F32).reshape(1, nh), a_row)


LOG2E = 1.4426950408889634
DT_FLOOR = 1e-37


def _split3(v):
    hi = v.astype(BF16)
    rest = v - hi.astype(F32)
    mid = rest.astype(BF16)
    lo = (rest - mid.astype(F32)).astype(BF16)
    return hi, mid, lo


def _ssd_kernel(xs_ref, b_ref, c_ref, z_ref, dt_ref, da_ref,
                wx_ref, wb_ref, wc_ref, cbx_ref, cbb_ref, cbc_ref,
                dskip_ref, normw_ref, expand_ref, o_ref,
                state_sc, carry_ref, *, nsub):
    L = CHUNK
    nh = HEADS_PER_SSM_GROUP
    hp = lax.Precision.HIGHEST
    nt_dims = (((1,), (1,)), ((), ()))

    @pl.when(pl.program_id(2) == 0)
    def _():
        state_sc[...] = jnp.zeros_like(state_sc)
        carry_ref[...] = jnp.zeros_like(carry_ref)

    row_i = lax.broadcasted_iota(jnp.int32, (L, L), 0)
    col_i = lax.broadcasted_iota(jnp.int32, (L, L), 1)
    causal = row_i >= col_i
    triu = (row_i <= col_i).astype(F32)
    eye = (row_i == col_i).astype(F32)
    eye_b = eye.astype(BF16)
    lane = lax.broadcasted_iota(jnp.int32, (L, LANES), 1)
    head_lo = (lane < SSM_HEAD_DIM).astype(F32).astype(BF16)
    head_hi = (lane >= SSM_HEAD_DIM).astype(F32).astype(BF16)

    shift_mat = jnp.concatenate(
        [(col_i == row_i - s).astype(F32).astype(BF16) for s in range(1, CONV_WIDTH)],
        axis=0)
    conv_w = jnp.concatenate([wx_ref[...], wb_ref[...], wc_ref[...]], axis=1)
    conv_bias = jnp.concatenate([cbx_ref[...], cbb_ref[...], cbc_ref[...]], axis=1)
    tail_rows = carry_ref.shape[0]
    tail_row = lax.broadcasted_iota(jnp.int32, carry_ref.shape, 0)

    def conv_silu(cur):
        cur_f = cur.astype(F32)
        tail = carry_ref[...]
        carry_ref[...] = cur_f[L - tail_rows:, :]
        shifted = jnp.dot(shift_mat, cur, preferred_element_type=F32)
        acc = conv_bias + conv_w[CONV_WIDTH - 1:CONV_WIDTH, :] * cur_f
        head = jnp.zeros_like(tail)
        for s in range(1, CONV_WIDTH):
            k = CONV_WIDTH - 1 - s
            acc = acc + conv_w[k:k + 1, :] * shifted[(s - 1) * L:s * L, :]
            head = head + conv_w[k:k + 1, :] * jnp.where(tail_row < s,
                                                          pltpu.roll(tail, s, 0), 0.0)
        acc = jnp.concatenate([acc[:tail_rows] + head, acc[tail_rows:]], axis=0)
        return _silu(acc)

    def per_chunk(j, carry):
        rows = pl.ds(pl.multiple_of(j * L, L), L)
        xbc = conv_silu(jnp.concatenate([xs_ref[rows, :], b_ref[rows, :], c_ref[rows, :]],
                                        axis=1))
        xs = xbc[:, :GROUP_CH]
        bm = xbc[:, GROUP_CH:GROUP_CH + D_STATE]
        cm = xbc[:, GROUP_CH + D_STATE:]
        xs_b, bm_b, cm_b = xs.astype(BF16), bm.astype(BF16), cm.astype(BF16)

        dt_t = dt_ref[:, rows]
        a_t = jnp.dot(da_ref[:, rows], triu, precision=hp,
                      preferred_element_type=F32) * LOG2E
        log2dt = jnp.log(jnp.maximum(dt_t, DT_FLOOR)) * LOG2E
        a_last = a_t[:, L - 1:L]
        r_t = a_t - log2dt
        e_t = jnp.exp2(a_t)
        w_t = jnp.exp2(a_last - r_t)
        a_cols = lax.dot_general(eye, a_t, nt_dims, precision=hp,
                                 preferred_element_type=F32)
        parts = _split3(jnp.concatenate([e_t, w_t], axis=0))
        parts = jnp.concatenate(parts + (jnp.zeros_like(parts[0]),), axis=0)
        cols3 = lax.dot_general(eye_b, parts, nt_dims,
                                preferred_element_type=F32).astype(BF16)
        ew = jnp.dot(cols3, expand_ref[...], preferred_element_type=F32)
        e_exp, w_exp = ew[:, :GROUP_CH], ew[:, GROUP_CH:]

        cb = lax.dot_general(cm_b, bm_b, nt_dims, preferred_element_type=F32)
        state = state_sc[...]
        z_off = jnp.dot(cm_b, state.astype(BF16), preferred_element_type=F32)

        ys = []
        for c in range(PAIRS):
            lanes = slice(c * LANES, (c + 1) * LANES)
            x_c = xs_b[:, lanes]
            rhs = jnp.concatenate([x_c * head_lo, x_c * head_hi], axis=0)
            lhs = []
            for h in (2 * c, 2 * c + 1):
                seg = a_cols[:, h:h + 1] - r_t[h:h + 1, :]
                lhs.append((jnp.exp2(jnp.where(causal, seg, -jnp.inf)) * cb).astype(BF16))
            y_c = jnp.dot(jnp.concatenate(lhs, axis=1), rhs, preferred_element_type=F32)
            ys.append(y_c + e_exp[:, lanes] * z_off[:, lanes])
        y = jnp.concatenate(ys, axis=1)

        upd = lax.dot_general(bm_b, (xs * w_exp).astype(BF16),
                              (((0,), (0,)), ((), ())), preferred_element_type=F32)
        state_sc[...] = state * e_exp[L - 1:L, :] + upd

        y = y + dskip_ref[...] * xs
        y = y * _silu(z_ref[rows, :].astype(F32))
        ms = jnp.mean(y * y, axis=-1, keepdims=True)
        y = y * lax.rsqrt(ms + RMS_EPS) * normw_ref[...]
        o_ref[rows, :] = y.astype(o_ref.dtype)
        return carry

    lax.fori_loop(0, nsub, per_chunk, 0)


def ssd_mixer_core(proj, dt_t, da_t, conv_w, conv_b, d_skip, norm_w, batch, seq, rows_per_step):
    m = proj.shape[0]
    d_inner = SSM_GROUPS * GROUP_CH
    t = rows_per_step
    nsteps = seq // t
    xcol0 = d_inner // GROUP_CH
    bcol0 = 2 * d_inner // D_STATE
    ccol0 = bcol0 + SSM_GROUPS
    wb0 = d_inner // D_STATE
    wc0 = wb0 + SSM_GROUPS
    conv_w = conv_w.astype(F32)
    conv_b = conv_b.astype(F32).reshape(1, -1)
    d_exp = jnp.repeat(d_skip.astype(F32), SSM_HEAD_DIM).reshape(1, d_inner)
    norm_w = norm_w.astype(F32).reshape(1, d_inner)
    head_exp = jnp.repeat(jnp.eye(HEADS_PER_SSM_GROUP, dtype=F32), SSM_HEAD_DIM, axis=1)
    zeros = jnp.zeros_like(head_exp)
    pair = jnp.block([[head_exp, zeros], [zeros, head_exp]])
    expand = jnp.concatenate([pair, pair, pair, jnp.zeros_like(pair)], axis=0).astype(BF16)

    def row(bi, s):
        return bi * nsteps + s

    in_specs = [
        pl.BlockSpec((t, GROUP_CH), lambda bi, g, s: (row(bi, s), xcol0 + g)),
        pl.BlockSpec((t, D_STATE), lambda bi, g, s: (row(bi, s), bcol0 + g)),
        pl.BlockSpec((t, D_STATE), lambda bi, g, s: (row(bi, s), ccol0 + g)),
        pl.BlockSpec((t, GROUP_CH), lambda bi, g, s: (row(bi, s), g)),
        pl.BlockSpec((HEADS_PER_SSM_GROUP, t), lambda bi, g, s: (g, row(bi, s))),
        pl.BlockSpec((HEADS_PER_SSM_GROUP, t), lambda bi, g, s: (g, row(bi, s))),
        pl.BlockSpec((CONV_WIDTH, GROUP_CH), lambda bi, g, s: (0, g)),
        pl.BlockSpec((CONV_WIDTH, D_STATE), lambda bi, g, s: (0, wb0 + g)),
        pl.BlockSpec((CONV_WIDTH, D_STATE), lambda bi, g, s: (0, wc0 + g)),
        pl.BlockSpec((1, GROUP_CH), lambda bi, g, s: (0, g)),
        pl.BlockSpec((1, D_STATE), lambda bi, g, s: (0, wb0 + g)),
        pl.BlockSpec((1, D_STATE), lambda bi, g, s: (0, wc0 + g)),
        pl.BlockSpec((1, GROUP_CH), lambda bi, g, s: (0, g)),
        pl.BlockSpec((1, GROUP_CH), lambda bi, g, s: (0, g)),
        pl.BlockSpec(expand.shape, lambda bi, g, s: (0, 0)),
    ]
    return pl.pallas_call(
        functools.partial(_ssd_kernel, nsub=t // CHUNK),
        grid=(batch, SSM_GROUPS, nsteps),
        in_specs=in_specs,
        out_specs=pl.BlockSpec((t, GROUP_CH), lambda bi, g, s: (row(bi, s), g)),
        out_shape=jax.ShapeDtypeStruct((m, d_inner), BF16),
        scratch_shapes=[pltpu.VMEM((D_STATE, GROUP_CH), F32),
                        pltpu.VMEM((CONV_TAIL_ROWS, GROUP_CH + 2 * D_STATE), F32)],
        compiler_params=_cparams("parallel", "parallel", "arbitrary"),
        name="ssd_mixer_core",
    )(proj, proj, proj, proj, dt_t, da_t, conv_w, conv_w, conv_w, conv_b, conv_b, conv_b,
      d_exp, norm_w, expand)


def _attention_layer(x, w_in, w_out, rel_bias, gamma, beta, alpha, emit_bf16):
    batch, seq, d_model = x.shape
    dilations = tuple(d for _, d in ATTN_PATTERNS)
    group_cols = 3 * D_ATTN
    gate_col0 = N_GROUPS_ATTN * group_cols
    wb = w_in.astype(BF16)
    xperm = cast_permute(x, dilations, bm=256)
    planes = []
    for g, (bm, bn) in enumerate(ATTN_PROJ_TILES):
        first = g * group_cols // bn
        if g == 0:
            nqkv, gate_first = group_cols // bn, gate_col0 // bn
            n = group_cols + D_ATTN
            col_block = lambda j, nqkv=nqkv, gate_first=gate_first: jnp.where(
                j < nqkv, j, j - nqkv + gate_first)
        else:
            n = group_cols
            col_block = lambda j, first=first: first + j
        planes.append(proj_planes(xperm[g], wb, col_block, n, D_ATTN, HEAD_DIM ** -0.5, bm, bn))
    y = dilated_attention(planes, masked_bias(rel_bias), batch, seq)
    return outproj_layernorm(y, w_out.astype(BF16), x.reshape(batch * seq, d_model),
                             gamma, beta, alpha, bm=512, bk=512, emit_bf16=emit_bf16)


def _ssd_layer(x, xb, w_in, conv_w, conv_b, dt_bias, a_log, d_skip, norm_w, w_out,
               gamma, beta, alpha):
    batch, seq, d_model = x.shape
    m = batch * seq
    n_main = w_in.shape[1] - dt_bias.shape[0]
    xf = x.reshape(m, d_model)
    proj = proj_flat(xb.reshape(m, d_model), w_in.astype(BF16), n_main, bm=1024, bn=1024)
    dt_t, da_t = dt_projection(xf, w_in, dt_bias, a_log, bm=512)
    y = ssd_mixer_core(proj, dt_t, da_t, conv_w, conv_b, d_skip, norm_w, batch, seq,
                       rows_per_step=512)
    return outproj_layernorm(y, w_out.astype(BF16), xf, gamma, beta, alpha,
                             bm=512, bk=512, emit_bf16=False)


def kernel(x, w_in_attn, w_out_attn, rel_bias, w_in_ssm, conv_w, conv_b, dt_bias, a_log,
           d_skip, ssm_norm_w, w_out_ssm, ln_g, ln_b):
    batch, seq, d_model = x.shape
    depth = ln_g.shape[0]
    alpha = (2 * depth) ** 0.25
    xb = None
    for i in range(depth):
        j = i // 2
        if i % 2 == 0:
            outs = _attention_layer(x, w_in_attn[j], w_out_attn[j], rel_bias,
                                    ln_g[i], ln_b[i], alpha, emit_bf16=i + 1 < depth)
            if i + 1 < depth:
                xb = outs[1].reshape(batch, seq, d_model)
        else:
            outs = _ssd_layer(x, xb, w_in_ssm[j], conv_w[j], conv_b[j], dt_bias[j], a_log[j],
                              d_skip[j], ssm_norm_w[j], w_out_ssm[j], ln_g[i], ln_b[i], alpha)
        x = outs[0].reshape(batch, seq, d_model)
    return x
```

```python
import functools
import math

import jax
import jax.numpy as jnp
from jax import lax
from jax.experimental import pallas as pl
from jax.experimental.pallas import tpu as pltpu

F32 = jnp.float32
BF16 = jnp.bfloat16

LANES = 128
VMEM_LIMIT_BYTES = 56 * 1024 * 1024

ATTN_PATTERNS = ((128, 1), (512, 4), (2048, 16))
N_GROUPS_ATTN = 3
HEADS_PER_GROUP = 16
HEAD_DIM = 128
D_ATTN = HEADS_PER_GROUP * HEAD_DIM
ATTN_BLOCK = 128
NUM_BUCKETS = 32
MAX_DISTANCE = 2048
SSM_HEAD_DIM = 64
SSM_GROUPS = 8
HEADS_PER_SSM_GROUP = 16
D_STATE = 128
CONV_WIDTH = 4
CHUNK = 128
LN_EPS = 1e-5
RMS_EPS = 1e-5
NEG_INF = -1e30

GROUP_CH = HEADS_PER_SSM_GROUP * SSM_HEAD_DIM
PAIRS = GROUP_CH // LANES
CONV_TAIL_ROWS = 8


def _cparams(*sem):
    return pltpu.CompilerParams(dimension_semantics=sem, vmem_limit_bytes=VMEM_LIMIT_BYTES)


def _cast_permute_kernel(x_ref, *o_refs, dilations):
    xb = x_ref[...].astype(BF16)
    bm = xb.shape[0]
    dst = lax.broadcasted_iota(jnp.int32, (bm, bm), 0)
    src = lax.broadcasted_iota(jnp.int32, (bm, bm), 1)
    for o_ref, d in zip(o_refs, dilations):
        rows = bm // d
        if d == 1:
            o_ref[0] = xb
            continue
        perm = (src == (dst % rows) * d + dst // rows).astype(BF16)
        xp = jnp.dot(perm, xb, preferred_element_type=F32).astype(BF16)
        for r in range(d):
            o_ref[r] = xp[r * rows:(r + 1) * rows, :]


def cast_permute(x, dilations, bm):
    b, s, d_model = x.shape
    return pl.pallas_call(
        functools.partial(_cast_permute_kernel, dilations=dilations),
        grid=(b, s // bm),
        in_specs=[pl.BlockSpec((None, bm, d_model), lambda bi, i: (bi, i, 0))],
        out_specs=[pl.BlockSpec((None, d, bm // d, d_model), lambda bi, i: (bi, 0, i, 0))
                   for d in dilations],
        out_shape=[jax.ShapeDtypeStruct((b, d, s // d, d_model), BF16) for d in dilations],
        compiler_params=_cparams("parallel", "parallel"),
        name="cast_permute",
    )(x)


def _proj_planes_kernel(x_ref, w_ref, o_ref, *, q_blocks, q_scale):
    acc = jnp.dot(x_ref[...], w_ref[...], preferred_element_type=F32)
    acc = acc * jnp.where(pl.program_id(0) < q_blocks, q_scale, 1.0)
    for c in range(o_ref.shape[0]):
        o_ref[c] = acc[:, c * LANES:(c + 1) * LANES].astype(o_ref.dtype)


def proj_planes(xp, w, col_block, n, q_cols, q_scale, bm, bn):
    b, dilation, sub_len, d_model = xp.shape
    bm = min(bm, sub_len)
    nbl = sub_len // bm
    grid = (n // bn, b, dilation, nbl)
    return pl.pallas_call(
        functools.partial(_proj_planes_kernel, q_blocks=q_cols // bn, q_scale=q_scale),
        grid=grid,
        in_specs=[
            pl.BlockSpec((None, None, bm, d_model), lambda j, bi, r, i: (bi, r, i, 0)),
            pl.BlockSpec((d_model, bn), lambda j, bi, r, i: (0, col_block(j))),
        ],
        out_specs=pl.BlockSpec(
            (bn // LANES, bm, LANES),
            lambda j, bi, r, i: (j, (bi * dilation + r) * nbl + i, 0)),
        out_shape=jax.ShapeDtypeStruct((n // LANES, b * dilation * sub_len, LANES), BF16),
        compiler_params=_cparams("parallel", "parallel", "parallel", "parallel"),
        name=f"proj_planes_d{dilation}",
    )(xp, w)


def _proj_flat_kernel(x_ref, w_ref, o_ref):
    o_ref[...] = jnp.dot(x_ref[...], w_ref[...],
                         preferred_element_type=F32).astype(o_ref.dtype)


def proj_flat(xb, w, n_out, bm, bn):
    m, d_model = xb.shape
    return pl.pallas_call(
        _proj_flat_kernel,
        grid=(n_out // bn, m // bm),
        in_specs=[
            pl.BlockSpec((bm, d_model), lambda j, i: (i, 0)),
            pl.BlockSpec((d_model, bn), lambda j, i: (0, j)),
        ],
        out_specs=pl.BlockSpec((bm, bn), lambda j, i: (i, j)),
        out_shape=jax.ShapeDtypeStruct((m, n_out), BF16),
        compiler_params=_cparams("parallel", "parallel"),
        name="proj_flat",
    )(xb, w)


def _t5_causal_bucket(dist):
    max_exact = NUM_BUCKETS // 2
    d_f = jnp.maximum(dist, 1).astype(F32)
    large = max_exact + (jnp.log(d_f / max_exact) / math.log(MAX_DISTANCE / max_exact)
                         * (NUM_BUCKETS - max_exact)).astype(jnp.int32)
    large = jnp.minimum(large, NUM_BUCKETS - 1)
    return jnp.where(dist < max_exact, dist, large)


def _band_buckets():
    qi = jnp.arange(ATTN_BLOCK)[:, None]
    ki = jnp.arange(2 * ATTN_BLOCK)[None, :]
    delta = ATTN_BLOCK + qi - ki
    out = []
    for window, dilation in ATTN_PATTERNS:
        span = window // dilation
        band = (delta >= 0) & (delta <= span)
        bucket = _t5_causal_bucket(jnp.clip(delta, 0, None) * dilation)
        out.append(jnp.where(band, bucket, -1))
    return jnp.stack(out).astype(jnp.int32)


def _bias_kernel(table_ref, bucket_ref, o_ref):
    g = pl.program_id(0)
    bucket = bucket_ref[...]
    prev_half = lax.broadcasted_iota(jnp.int32, bucket.shape, 1) < ATTN_BLOCK

    def per_head(h, carry):
        acc = jnp.full(bucket.shape, NEG_INF, F32)
        for b in range(NUM_BUCKETS):
            acc = jnp.where(bucket == b, table_ref[b, g * HEADS_PER_GROUP + h], acc)
        o_ref[h, 0] = jnp.where(prev_half, NEG_INF, acc)
        o_ref[h, 1] = acc
        return carry

    lax.fori_loop(0, HEADS_PER_GROUP, per_head, 0)


def masked_bias(rel_bias):
    buckets = _band_buckets()
    return pl.pallas_call(
        _bias_kernel,
        grid=(N_GROUPS_ATTN,),
        in_specs=[
            pl.BlockSpec(memory_space=pltpu.SMEM),
            pl.BlockSpec((None, ATTN_BLOCK, 2 * ATTN_BLOCK), lambda g: (g, 0, 0)),
        ],
        out_specs=pl.BlockSpec((None, HEADS_PER_GROUP, 2, ATTN_BLOCK, 2 * ATTN_BLOCK),
                               lambda g: (g, 0, 0, 0, 0)),
        out_shape=jax.ShapeDtypeStruct(
            (N_GROUPS_ATTN, HEADS_PER_GROUP, 2, ATTN_BLOCK, 2 * ATTN_BLOCK), F32),
        compiler_params=_cparams("parallel"),
        name="masked_bias",
    )(rel_bias.astype(F32), buckets)


def _silu(v):
    return v * (1.0 / (1.0 + jnp.exp(-v)))


ATTN_UNROLL = 16
ATTN_PROJ_TILES = ((1024, 1024), (1024, 1024), (256, 2048))


def _attn_kernel(q0, k0, v0, q1, k1, v1, q2, k2, v2, gate_ref, bias_ref, y_ref, o_sc, lse_sc):
    seq = y_ref.shape[0]
    blk = ATTN_BLOCK
    qkv = ((q0, k0, v0), (q1, k1, v1), (q2, k2, v2))

    def finish(g, dilation, tok0, s, vals):
        m = jnp.max(s, axis=-1, keepdims=True)
        p = jnp.exp(s - m)
        denom = jnp.sum(p, axis=-1, keepdims=True)
        acc = jnp.dot(p.astype(BF16), vals, preferred_element_type=F32)
        o = acc * (1.0 / denom)
        lse = jnp.broadcast_to(m + jnp.log(denom), (blk, LANES))
        if dilation == 1:
            rows = pl.ds(tok0, blk)
        else:
            rows = pl.ds(tok0, blk, stride=dilation)
        o_sc[g, rows, :] = o
        lse_sc[g, rows, :] = lse

    def scores(q, keys):
        return lax.dot_general(q, keys, (((1,), (1,)), ((), ())), preferred_element_type=F32)

    for g, (_, dilation) in enumerate(ATTN_PATTERNS):
        q_ref, k_ref, v_ref = qkv[g]
        nb = seq // dilation // blk

        def per_block(idx, carry, g=g, dilation=dilation, q_ref=q_ref, k_ref=k_ref,
                      v_ref=v_ref, nb=nb):
            r, n = idx // nb, idx % nb
            row0 = pl.multiple_of(idx * blk, blk)
            prev0 = pl.multiple_of(jnp.maximum(row0 - blk, 0), blk)
            keys = jnp.concatenate([k_ref[pl.ds(prev0, blk), :],
                                    k_ref[pl.ds(row0, blk), :]], axis=0)
            vals = jnp.concatenate([v_ref[pl.ds(prev0, blk), :],
                                    v_ref[pl.ds(row0, blk), :]], axis=0)
            s = scores(q_ref[pl.ds(row0, blk), :], keys) + bias_ref[g, jnp.minimum(n, 1)]
            finish(g, dilation, r + dilation * n * blk, s, vals)
            return carry

        lax.fori_loop(0, dilation * nb, per_block, 0, unroll=ATTN_UNROLL)

    def combine(i, carry):
        rows = pl.ds(pl.multiple_of(i * blk, blk), blk)
        l0, l1, l2 = lse_sc[0, rows, :], lse_sc[1, rows, :], lse_sc[2, rows, :]
        m = jnp.maximum(jnp.maximum(l0, l1), l2)
        w0, w1, w2 = jnp.exp(l0 - m), jnp.exp(l1 - m), jnp.exp(l2 - m)
        o = (w0 * o_sc[0, rows, :] + w1 * o_sc[1, rows, :] + w2 * o_sc[2, rows, :])
        o = o * (1.0 / (w0 + w1 + w2))
        gate = gate_ref[rows, :].astype(F32)
        y_ref[rows, :] = (o * _silu(gate)).astype(y_ref.dtype)
        return carry

    lax.fori_loop(0, seq // blk, combine, 0)


def dilated_attention(planes, bias, batch, seq):
    hpg = HEADS_PER_GROUP

    def plane_spec(offset):
        return pl.BlockSpec((None, seq, LANES), lambda bi, h: (offset + h, bi, 0))

    in_specs, args = [], []
    for g in range(N_GROUPS_ATTN):
        for part in range(3):
            in_specs.append(plane_spec(part * hpg))
            args.append(planes[g])
    in_specs.append(plane_spec(3 * hpg))
    args.append(planes[0])
    in_specs.append(pl.BlockSpec((N_GROUPS_ATTN, None, 2, ATTN_BLOCK, 2 * ATTN_BLOCK),
                                 lambda bi, h: (0, h, 0, 0, 0)))
    args.append(bias)
    return pl.pallas_call(
        _attn_kernel,
        grid=(batch, hpg),
        in_specs=in_specs,
        out_specs=pl.BlockSpec((seq, LANES), lambda bi, h: (bi, h)),
        out_shape=jax.ShapeDtypeStruct((batch * seq, D_ATTN), BF16),
        scratch_shapes=[pltpu.VMEM((N_GROUPS_ATTN, seq, LANES), F32),
                        pltpu.VMEM((N_GROUPS_ATTN, seq, LANES), F32)],
        compiler_params=_cparams("parallel", "parallel"),
        name="dilated_attention",
    )(*args)


LN_ROWS = 32


def _outproj_ln_kernel(y_ref, w_ref, x_ref, g_ref, b_ref, *refs, alpha, nt, nk):
    o_refs, accs = refs[:-2], refs[-2:]
    i, k = pl.program_id(0), pl.program_id(1)
    rc = x_ref.shape[0]
    d_model = w_ref.shape[1]
    n_ln = rc // LN_ROWS
    n_mm = n_ln if d_model % (n_ln * 2 * LANES) == 0 else 1
    cols = d_model // n_mm

    def matmul_part(acc_ref, first, c):
        sl = slice(c * cols, (c + 1) * cols)
        d = jnp.dot(y_ref[...], w_ref[:, sl], preferred_element_type=F32)
        if first:
            acc_ref[:, sl] = d
        else:
            acc_ref[:, sl] += d

    def layer_norm_part(done_ref, j):
        sub = pl.ds(j * LN_ROWS, LN_ROWS)
        src = pl.ds(pl.multiple_of(k * rc + j * LN_ROWS, LN_ROWS), LN_ROWS)
        v = done_ref[src, :] + alpha * x_ref[sub, :]
        mu = jnp.mean(v, axis=-1, keepdims=True)
        c = v - mu
        var = jnp.mean(c * c, axis=-1, keepdims=True)
        out = c * lax.rsqrt(var + LN_EPS) * g_ref[...] + b_ref[...]
        o_refs[0][sub, :] = out
        if len(o_refs) > 1:
            o_refs[1][sub, :] = out.astype(BF16)

    for parity in (0, 1):
        acc_ref, done_ref = accs[parity], accs[1 - parity]
        mine = (i % 2 == parity)
        for first in (True, False):
            cond = mine & ((k == 0) if first else (k > 0))

            if parity == 0:
                @pl.when(cond & (i == 0))
                def _(acc_ref=acc_ref, first=first):
                    for c in range(n_mm):
                        matmul_part(acc_ref, first, c)

            @pl.when(cond & (i > 0) & (i < nt))
            def _(acc_ref=acc_ref, done_ref=done_ref, first=first):
                for step in range(max(n_ln, n_mm)):
                    if step < n_mm:
                        matmul_part(acc_ref, first, step)
                    if step < n_ln:
                        layer_norm_part(done_ref, step)

        if parity == nt % 2:
            @pl.when(i == nt)
            def _(done_ref=done_ref):
                for j in range(n_ln):
                    layer_norm_part(done_ref, j)


def outproj_layernorm(y, w, x_res, gamma, beta, alpha, bm, bk, emit_bf16):
    m, kdim = y.shape
    d_model = w.shape[1]
    nt, nk = m // bm, kdim // bk
    assert nk >= 2
    rc = bm // nk

    def chunk(i, k):
        return (jnp.where(i == 0, 0, (i - 1) * nk + k), 0)

    out_shape = [jax.ShapeDtypeStruct((m, d_model), F32)]
    out_specs = [pl.BlockSpec((rc, d_model), chunk)]
    if emit_bf16:
        out_shape.append(jax.ShapeDtypeStruct((m, d_model), BF16))
        out_specs.append(pl.BlockSpec((rc, d_model), chunk))
    return pl.pallas_call(
        functools.partial(_outproj_ln_kernel, alpha=alpha, nt=nt, nk=nk),
        grid=(nt + 1, nk),
        in_specs=[
            pl.BlockSpec((bm, bk), lambda i, k: (jnp.minimum(i, nt - 1), k)),
            pl.BlockSpec((bk, d_model), lambda i, k: (jnp.where(i == nt, 0, k), 0)),
            pl.BlockSpec((rc, d_model), chunk),
            pl.BlockSpec((1, d_model), lambda i, k: (0, 0)),
            pl.BlockSpec((1, d_model), lambda i, k: (0, 0)),
        ],
        out_specs=out_specs,
        out_shape=out_shape,
        scratch_shapes=[pltpu.VMEM((bm, d_model), F32), pltpu.VMEM((bm, d_model), F32)],
        compiler_params=_cparams("arbitrary", "arbitrary"),
        name="outproj_layernorm",
    )(y, w, x_res, gamma.reshape(1, d_model).astype(F32), beta.reshape(1, d_model).astype(F32))


def _dt_kernel(x_ref, w_ref, bias_ref, a_ref, dt_ref, da_ref):
    w = w_ref[...]
    w_hi = w.astype(BF16)
    w_lo = (w - w_hi.astype(F32)).astype(BF16)
    x = x_ref[...]
    raw = (jnp.dot(x, w_hi, preferred_element_type=F32)
           + jnp.dot(x, w_lo, preferred_element_type=F32))
    v = raw + bias_ref[...]
    dt = jnp.maximum(v, 0.0) + jnp.log1p(jnp.exp(-jnp.abs(v)))
    dt_ref[...] = dt.T
    da_ref[...] = (dt * a_ref[...]).T


def dt_projection(x, w_in, dt_bias, a_log, bm):
    m, d_model = x.shape
    nh = dt_bias.shape[0]
    dt_block = w_in.shape[1] // nh - 1
    a_row = (-jnp.exp(a_log.astype(F32))).reshape(1, nh)
    return pl.pallas_call(
        _dt_kernel,
        grid=(m // bm,),
        in_specs=[
            pl.BlockSpec((bm, d_model), lambda i: (i, 0)),
            pl.BlockSpec((d_model, nh), lambda i: (0, dt_block)),
            pl.BlockSpec((1, nh), lambda i: (0, 0)),
            pl.BlockSpec((1, nh), lambda i: (0, 0)),
        ],
        out_specs=[pl.BlockSpec((nh, bm), lambda i: (0, i)),
                   pl.BlockSpec((nh, bm), lambda i: (0, i))],
        out_shape=[jax.ShapeDtypeStruct((nh, m), F32), jax.ShapeDtypeStruct((nh, m), F32)],
        compiler_params=_cparams("parallel"),
        name="dt_projection",
    )(x, w_in, dt_bias.astype(F32).reshape(1, nh), a_row)


LOG2E = 1.4426950408889634
DT_FLOOR = 1e-37


def _split3(v):
    hi = v.astype(BF16)
    rest = v - hi.astype(F32)
    mid = rest.astype(BF16)
    lo = (rest - mid.astype(F32)).astype(BF16)
    return hi, mid, lo


def _ssd_kernel(xs_ref, b_ref, c_ref, z_ref, dt_ref, da_ref,
                wx_ref, wb_ref, wc_ref, cbx_ref, cbb_ref, cbc_ref,
                dskip_ref, normw_ref, expand_ref, o_ref,
                state_sc, carry_ref, *, nsub):
    L = CHUNK
    nh = HEADS_PER_SSM_GROUP
    hp = lax.Precision.HIGHEST
    nt_dims = (((1,), (1,)), ((), ()))

    @pl.when(pl.program_id(2) == 0)
    def _():
        state_sc[...] = jnp.zeros_like(state_sc)
        carry_ref[...] = jnp.zeros_like(carry_ref)

    row_i = lax.broadcasted_iota(jnp.int32, (L, L), 0)
    col_i = lax.broadcasted_iota(jnp.int32, (L, L), 1)
    causal = row_i >= col_i
    triu = (row_i <= col_i).astype(F32)
    eye = (row_i == col_i).astype(F32)
    eye_b = eye.astype(BF16)
    lane = lax.broadcasted_iota(jnp.int32, (L, LANES), 1)
    head_lo = (lane < SSM_HEAD_DIM).astype(F32).astype(BF16)
    head_hi = (lane >= SSM_HEAD_DIM).astype(F32).astype(BF16)

    shift_mat = jnp.concatenate(
        [(col_i == row_i - s).astype(F32).astype(BF16) for s in range(1, CONV_WIDTH)],
        axis=0)
    conv_w = jnp.concatenate([wx_ref[...], wb_ref[...], wc_ref[...]], axis=1)
    conv_bias = jnp.concatenate([cbx_ref[...], cbb_ref[...], cbc_ref[...]], axis=1)
    tail_rows = carry_ref.shape[0]
    tail_row = lax.broadcasted_iota(jnp.int32, carry_ref.shape, 0)

    def conv_silu(cur):
        cur_f = cur.astype(F32)
        tail = carry_ref[...]
        carry_ref[...] = cur_f[L - tail_rows:, :]
        shifted = jnp.dot(shift_mat, cur, preferred_element_type=F32)
        acc = conv_bias + conv_w[CONV_WIDTH - 1:CONV_WIDTH, :] * cur_f
        head = jnp.zeros_like(tail)
        for s in range(1, CONV_WIDTH):
            k = CONV_WIDTH - 1 - s
            acc = acc + conv_w[k:k + 1, :] * shifted[(s - 1) * L:s * L, :]
            head = head + conv_w[k:k + 1, :] * jnp.where(tail_row < s,
                                                          pltpu.roll(tail, s, 0), 0.0)
        acc = jnp.concatenate([acc[:tail_rows] + head, acc[tail_rows:]], axis=0)
        return _silu(acc)

    def per_chunk(j, carry):
        rows = pl.ds(pl.multiple_of(j * L, L), L)
        xbc = conv_silu(jnp.concatenate([xs_ref[rows, :], b_ref[rows, :], c_ref[rows, :]],
                                        axis=1))
        xs = xbc[:, :GROUP_CH]
        bm = xbc[:, GROUP_CH:GROUP_CH + D_STATE]
        cm = xbc[:, GROUP_CH + D_STATE:]
        xs_b, bm_b, cm_b = xs.astype(BF16), bm.astype(BF16), cm.astype(BF16)

        dt_t = dt_ref[:, rows]
        a_t = jnp.dot(da_ref[:, rows], triu, precision=hp,
                      preferred_element_type=F32) * LOG2E
        log2dt = jnp.log(jnp.maximum(dt_t, DT_FLOOR)) * LOG2E
        a_last = a_t[:, L - 1:L]
        r_t = a_t - log2dt
        e_t = jnp.exp2(a_t)
        w_t = jnp.exp2(a_last - r_t)
        a_cols = lax.dot_general(eye, a_t, nt_dims, precision=hp,
                                 preferred_element_type=F32)
        parts = _split3(jnp.concatenate([e_t, w_t], axis=0))
        parts = jnp.concatenate(parts + (jnp.zeros_like(parts[0]),), axis=0)
        cols3 = lax.dot_general(eye_b, parts, nt_dims,
                                preferred_element_type=F32).astype(BF16)
        ew = jnp.dot(cols3, expand_ref[...], preferred_element_type=F32)
        e_exp, w_exp = ew[:, :GROUP_CH], ew[:, GROUP_CH:]

        cb = lax.dot_general(cm_b, bm_b, nt_dims, preferred_element_type=F32)
        state = state_sc[...]
        z_off = jnp.dot(cm_b, state.astype(BF16), preferred_element_type=F32)

        ys = []
        for c in range(PAIRS):
            lanes = slice(c * LANES, (c + 1) * LANES)
            x_c = xs_b[:, lanes]
            rhs = jnp.concatenate([x_c * head_lo, x_c * head_hi], axis=0)
            lhs = []
            for h in (2 * c, 2 * c + 1):
                seg = a_cols[:, h:h + 1] - r_t[h:h + 1, :]
                lhs.append((jnp.exp2(jnp.where(causal, seg, -jnp.inf)) * cb).astype(BF16))
            y_c = jnp.dot(jnp.concatenate(lhs, axis=1), rhs, preferred_element_type=F32)
            ys.append(y_c + e_exp[:, lanes] * z_off[:, lanes])
        y = jnp.concatenate(ys, axis=1)

        upd = lax.dot_general(bm_b, (xs * w_exp).astype(BF16),
                              (((0,), (0,)), ((), ())), preferred_element_type=F32)
        state_sc[...] = state * e_exp[L - 1:L, :] + upd

        y = y + dskip_ref[...] * xs
        y = y * _silu(z_ref[rows, :].astype(F32))
        ms = jnp.mean(y * y, axis=-1, keepdims=True)
        y = y * lax.rsqrt(ms + RMS_EPS) * normw_ref[...]
        o_ref[rows, :] = y.astype(o_ref.dtype)
        return carry

    lax.fori_loop(0, nsub, per_chunk, 0, unroll=2)


def ssd_mixer_core(proj, dt_t, da_t, conv_w, conv_b, d_skip, norm_w, batch, seq, rows_per_step):
    m = proj.shape[0]
    d_inner = SSM_GROUPS * GROUP_CH
    t = rows_per_step
    nsteps = seq // t
    xcol0 = d_inner // GROUP_CH
    bcol0 = 2 * d_inner // D_STATE
    ccol0 = bcol0 + SSM_GROUPS
    wb0 = d_inner // D_STATE
    wc0 = wb0 + SSM_GROUPS
    conv_w = conv_w.astype(F32)
    conv_b = conv_b.astype(F32).reshape(1, -1)
    d_exp = jnp.repeat(d_skip.astype(F32), SSM_HEAD_DIM).reshape(1, d_inner)
    norm_w = norm_w.astype(F32).reshape(1, d_inner)
    head_exp = jnp.repeat(jnp.eye(HEADS_PER_SSM_GROUP, dtype=F32), SSM_HEAD_DIM, axis=1)
    zeros = jnp.zeros_like(head_exp)
    pair = jnp.block([[head_exp, zeros], [zeros, head_exp]])
    expand = jnp.concatenate([pair, pair, pair, jnp.zeros_like(pair)], axis=0).astype(BF16)

    def row(bi, s):
        return bi * nsteps + s

    in_specs = [
        pl.BlockSpec((t, GROUP_CH), lambda bi, g, s: (row(bi, s), xcol0 + g)),
        pl.BlockSpec((t, D_STATE), lambda bi, g, s: (row(bi, s), bcol0 + g)),
        pl.BlockSpec((t, D_STATE), lambda bi, g, s: (row(bi, s), ccol0 + g)),
        pl.BlockSpec((t, GROUP_CH), lambda bi, g, s: (row(bi, s), g)),
        pl.BlockSpec((HEADS_PER_SSM_GROUP, t), lambda bi, g, s: (g, row(bi, s))),
        pl.BlockSpec((HEADS_PER_SSM_GROUP, t), lambda bi, g, s: (g, row(bi, s))),
        pl.BlockSpec((CONV_WIDTH, GROUP_CH), lambda bi, g, s: (0, g)),
        pl.BlockSpec((CONV_WIDTH, D_STATE), lambda bi, g, s: (0, wb0 + g)),
        pl.BlockSpec((CONV_WIDTH, D_STATE), lambda bi, g, s: (0, wc0 + g)),
        pl.BlockSpec((1, GROUP_CH), lambda bi, g, s: (0, g)),
        pl.BlockSpec((1, D_STATE), lambda bi, g, s: (0, wb0 + g)),
        pl.BlockSpec((1, D_STATE), lambda bi, g, s: (0, wc0 + g)),
        pl.BlockSpec((1, GROUP_CH), lambda bi, g, s: (0, g)),
        pl.BlockSpec((1, GROUP_CH), lambda bi, g, s: (0, g)),
        pl.BlockSpec(expand.shape, lambda bi, g, s: (0, 0)),
    ]
    return pl.pallas_call(
        functools.partial(_ssd_kernel, nsub=t // CHUNK),
        grid=(batch, SSM_GROUPS, nsteps),
        in_specs=in_specs,
        out_specs=pl.BlockSpec((t, GROUP_CH), lambda bi, g, s: (row(bi, s), g)),
        out_shape=jax.ShapeDtypeStruct((m, d_inner), BF16),
        scratch_shapes=[pltpu.VMEM((D_STATE, GROUP_CH), F32),
                        pltpu.VMEM((CONV_TAIL_ROWS, GROUP_CH + 2 * D_STATE), F32)],
        compiler_params=_cparams("parallel", "parallel", "arbitrary"),
        name="ssd_mixer_core",
    )(proj, proj, proj, proj, dt_t, da_t, conv_w, conv_w, conv_w, conv_b, conv_b, conv_b,
      d_exp, norm_w, expand)


def _attention_layer(x, w_in, w_out, rel_bias, gamma, beta, alpha, emit_bf16):
    batch, seq, d_model = x.shape
    dilations = tuple(d for _, d in ATTN_PATTERNS)
    group_cols = 3 * D_ATTN
    gate_col0 = N_GROUPS_ATTN * group_cols
    wb = w_in.astype(BF16)
    xperm = cast_permute(x, dilations, bm=256)
    planes = []
    for g, (bm, bn) in enumerate(ATTN_PROJ_TILES):
        first = g * group_cols // bn
        if g == 0:
            nqkv, gate_first = group_cols // bn, gate_col0 // bn
            n = group_cols + D_ATTN
            col_block = lambda j, nqkv=nqkv, gate_first=gate_first: jnp.where(
                j < nqkv, j, j - nqkv + gate_first)
        else:
            n = group_cols
            col_block = lambda j, first=first: first + j
        planes.append(proj_planes(xperm[g], wb, col_block, n, D_ATTN, HEAD_DIM ** -0.5, bm, bn))
    y = dilated_attention(planes, masked_bias(rel_bias), batch, seq)
    return outproj_layernorm(y, w_out.astype(BF16), x.reshape(batch * seq, d_model),
                             gamma, beta, alpha, bm=512, bk=512, emit_bf16=emit_bf16)


def _ssd_layer(x, xb, w_in, conv_w, conv_b, dt_bias, a_log, d_skip, norm_w, w_out,
               gamma, beta, alpha):
    batch, seq, d_model = x.shape
    m = batch * seq
    n_main = w_in.shape[1] - dt_bias.shape[0]
    xf = x.reshape(m, d_model)
    proj = proj_flat(xb.reshape(m, d_model), w_in.astype(BF16), n_main, bm=1024, bn=1024)
    dt_t, da_t = dt_projection(xb.reshape(m, d_model), w_in, dt_bias, a_log, bm=1024)
    y = ssd_mixer_core(proj, dt_t, da_t, conv_w, conv_b, d_skip, norm_w, batch, seq,
                       rows_per_step=512)
    return outproj_layernorm(y, w_out.astype(BF16), xf, gamma, beta, alpha,
                             bm=512, bk=1024, emit_bf16=False)


def kernel(x, w_in_attn, w_out_attn, rel_bias, w_in_ssm, conv_w, conv_b, dt_bias, a_log,
           d_skip, ssm_norm_w, w_out_ssm, ln_g, ln_b):
    batch, seq, d_model = x.shape
    depth = ln_g.shape[0]
    alpha = (2 * depth) ** 0.25
    xb = None
    for i in range(depth):
        j = i // 2
        if i % 2 == 0:
            outs = _attention_layer(x, w_in_attn[j], w_out_attn[j], rel_bias,
                                    ln_g[i], ln_b[i], alpha, emit_bf16=i + 1 < depth)
            if i + 1 < depth:
                xb = outs[1].reshape(batch, seq, d_model)
        else:
            outs = _ssd_layer(x, xb, w_in_ssm[j], conv_w[j], conv_b[j], dt_bias[j], a_log[j],
                              d_skip[j], ssm_norm_w[j], w_out_ssm[j], ln_g[i], ln_b[i], alpha)
        x = outs[0].reshape(batch, seq, d_model)
    return x
```

```python
import functools
import math

import jax
import jax.numpy as jnp
from jax import lax
from jax.experimental import pallas as pl
from jax.experimental.pallas import tpu as pltpu

F32 = jnp.float32
BF16 = jnp.bfloat16

LANES = 128
VMEM_LIMIT_BYTES = 56 * 1024 * 1024

ATTN_PATTERNS = ((128, 1), (512, 4), (2048, 16))
N_GROUPS_ATTN = 3
HEADS_PER_GROUP = 16
HEAD_DIM = 128
D_ATTN = HEADS_PER_GROUP * HEAD_DIM
ATTN_BLOCK = 128
NUM_BUCKETS = 32
MAX_DISTANCE = 2048
SSM_HEAD_DIM = 64
SSM_GROUPS = 8
HEADS_PER_SSM_GROUP = 16
D_STATE = 128
CONV_WIDTH = 4
CHUNK = 128
LN_EPS = 1e-5
RMS_EPS = 1e-5
NEG_INF = -1e30

GROUP_CH = HEADS_PER_SSM_GROUP * SSM_HEAD_DIM
PAIRS = GROUP_CH // LANES
CONV_TAIL_ROWS = 8


def _cparams(*sem):
    return pltpu.CompilerParams(dimension_semantics=sem, vmem_limit_bytes=VMEM_LIMIT_BYTES)


def _cast_permute_kernel(x_ref, *o_refs, dilations):
    xb = x_ref[...].astype(BF16)
    bm = xb.shape[0]
    dst = lax.broadcasted_iota(jnp.int32, (bm, bm), 0)
    src = lax.broadcasted_iota(jnp.int32, (bm, bm), 1)
    for o_ref, d in zip(o_refs, dilations):
        rows = bm // d
        if d == 1:
            o_ref[0] = xb
            continue
        perm = (src == (dst % rows) * d + dst // rows).astype(BF16)
        xp = jnp.dot(perm, xb, preferred_element_type=F32).astype(BF16)
        for r in range(d):
            o_ref[r] = xp[r * rows:(r + 1) * rows, :]


def cast_permute(x, dilations, bm):
    b, s, d_model = x.shape
    return pl.pallas_call(
        functools.partial(_cast_permute_kernel, dilations=dilations),
        grid=(b, s // bm),
        in_specs=[pl.BlockSpec((None, bm, d_model), lambda bi, i: (bi, i, 0))],
        out_specs=[pl.BlockSpec((None, d, bm // d, d_model), lambda bi, i: (bi, 0, i, 0))
                   for d in dilations],
        out_shape=[jax.ShapeDtypeStruct((b, d, s // d, d_model), BF16) for d in dilations],
        compiler_params=_cparams("parallel", "parallel"),
        name="cast_permute",
    )(x)


def _proj_planes_kernel(x_ref, w_ref, o_ref, *, q_blocks, q_scale):
    acc = jnp.dot(x_ref[...], w_ref[...], preferred_element_type=F32)
    acc = acc * jnp.where(pl.program_id(0) < q_blocks, q_scale, 1.0)
    for c in range(o_ref.shape[0]):
        o_ref[c] = acc[:, c * LANES:(c + 1) * LANES].astype(o_ref.dtype)


def proj_planes(xp, w, col_block, n, q_cols, q_scale, bm, bn):
    b, dilation, sub_len, d_model = xp.shape
    bm = min(bm, sub_len)
    nbl = sub_len // bm
    grid = (n // bn, b, dilation, nbl)
    return pl.pallas_call(
        functools.partial(_proj_planes_kernel, q_blocks=q_cols // bn, q_scale=q_scale),
        grid=grid,
        in_specs=[
            pl.BlockSpec((None, None, bm, d_model), lambda j, bi, r, i: (bi, r, i, 0)),
            pl.BlockSpec((d_model, bn), lambda j, bi, r, i: (0, col_block(j))),
        ],
        out_specs=pl.BlockSpec(
            (bn // LANES, bm, LANES),
            lambda j, bi, r, i: (j, (bi * dilation + r) * nbl + i, 0)),
        out_shape=jax.ShapeDtypeStruct((n // LANES, b * dilation * sub_len, LANES), BF16),
        compiler_params=_cparams("parallel", "parallel", "parallel", "parallel"),
        name=f"proj_planes_d{dilation}",
    )(xp, w)


def _proj_flat_kernel(x_ref, w_ref, o_ref):
    o_ref[...] = jnp.dot(x_ref[...], w_ref[...],
                         preferred_element_type=F32).astype(o_ref.dtype)


def proj_flat(xb, w, n_out, bm, bn):
    m, d_model = xb.shape
    return pl.pallas_call(
        _proj_flat_kernel,
        grid=(n_out // bn, m // bm),
        in_specs=[
            pl.BlockSpec((bm, d_model), lambda j, i: (i, 0)),
            pl.BlockSpec((d_model, bn), lambda j, i: (0, j)),
        ],
        out_specs=pl.BlockSpec((bm, bn), lambda j, i: (i, j)),
        out_shape=jax.ShapeDtypeStruct((m, n_out), BF16),
        compiler_params=_cparams("parallel", "parallel"),
        name="proj_flat",
    )(xb, w)


def _t5_causal_bucket(dist):
    max_exact = NUM_BUCKETS // 2
    d_f = jnp.maximum(dist, 1).astype(F32)
    large = max_exact + (jnp.log(d_f / max_exact) / math.log(MAX_DISTANCE / max_exact)
                         * (NUM_BUCKETS - max_exact)).astype(jnp.int32)
    large = jnp.minimum(large, NUM_BUCKETS - 1)
    return jnp.where(dist < max_exact, dist, large)


def _band_buckets():
    qi = jnp.arange(ATTN_BLOCK)[:, None]
    ki = jnp.arange(2 * ATTN_BLOCK)[None, :]
    delta = ATTN_BLOCK + qi - ki
    out = []
    for window, dilation in ATTN_PATTERNS:
        span = window // dilation
        band = (delta >= 0) & (delta <= span)
        bucket = _t5_causal_bucket(jnp.clip(delta, 0, None) * dilation)
        out.append(jnp.where(band, bucket, -1))
    return jnp.stack(out).astype(jnp.int32)


def _bias_kernel(table_ref, bucket_ref, o_ref):
    g = pl.program_id(0)
    bucket = bucket_ref[...]
    prev_half = lax.broadcasted_iota(jnp.int32, bucket.shape, 1) < ATTN_BLOCK

    def per_head(h, carry):
        acc = jnp.full(bucket.shape, NEG_INF, F32)
        for b in range(NUM_BUCKETS):
            acc = jnp.where(bucket == b, table_ref[b, g * HEADS_PER_GROUP + h], acc)
        o_ref[h, 0] = jnp.where(prev_half, NEG_INF, acc)
        o_ref[h, 1] = acc
        return carry

    lax.fori_loop(0, HEADS_PER_GROUP, per_head, 0)


def masked_bias(rel_bias):
    buckets = _band_buckets()
    return pl.pallas_call(
        _bias_kernel,
        grid=(N_GROUPS_ATTN,),
        in_specs=[
            pl.BlockSpec(memory_space=pltpu.SMEM),
            pl.BlockSpec((None, ATTN_BLOCK, 2 * ATTN_BLOCK), lambda g: (g, 0, 0)),
        ],
        out_specs=pl.BlockSpec((None, HEADS_PER_GROUP, 2, ATTN_BLOCK, 2 * ATTN_BLOCK),
                               lambda g: (g, 0, 0, 0, 0)),
        out_shape=jax.ShapeDtypeStruct(
            (N_GROUPS_ATTN, HEADS_PER_GROUP, 2, ATTN_BLOCK, 2 * ATTN_BLOCK), F32),
        compiler_params=_cparams("parallel"),
        name="masked_bias",
    )(rel_bias.astype(F32), buckets)


def _silu(v):
    h = 0.5 * v
    return h + h * jnp.tanh(h)


ATTN_UNROLL = 32
ATTN_PROJ_TILES = ((1024, 1024), (1024, 1024), (256, 2048))


def _attn_kernel(q0, k0, v0, q1, k1, v1, q2, k2, v2, gate_ref, bias_ref, y_ref, o_sc, lse_sc):
    seq = y_ref.shape[0]
    blk = ATTN_BLOCK
    qkv = ((q0, k0, v0), (q1, k1, v1), (q2, k2, v2))

    def finish(g, dilation, tok0, s, vals):
        m = jnp.max(s, axis=-1, keepdims=True)
        p = jnp.exp(s - m)
        denom = jnp.sum(p, axis=-1, keepdims=True)
        acc = jnp.dot(p.astype(BF16), vals, preferred_element_type=F32)
        o = acc * (1.0 / denom)
        lse = jnp.broadcast_to(m + jnp.log(denom), (blk, LANES))
        if dilation == 1:
            rows = pl.ds(tok0, blk)
        else:
            rows = pl.ds(tok0, blk, stride=dilation)
        o_sc[g, rows, :] = o
        lse_sc[g, rows, :] = lse

    def scores(q, keys):
        return lax.dot_general(q, keys, (((1,), (1,)), ((), ())), preferred_element_type=F32)

    for g, (_, dilation) in enumerate(ATTN_PATTERNS):
        q_ref, k_ref, v_ref = qkv[g]
        nb = seq // dilation // blk

        def per_block(idx, carry, g=g, dilation=dilation, q_ref=q_ref, k_ref=k_ref,
                      v_ref=v_ref, nb=nb):
            r, n = idx // nb, idx % nb
            row0 = pl.multiple_of(idx * blk, blk)
            prev0 = pl.multiple_of(jnp.maximum(row0 - blk, 0), blk)
            keys = jnp.concatenate([k_ref[pl.ds(prev0, blk), :],
                                    k_ref[pl.ds(row0, blk), :]], axis=0)
            vals = jnp.concatenate([v_ref[pl.ds(prev0, blk), :],
                                    v_ref[pl.ds(row0, blk), :]], axis=0)
            s = scores(q_ref[pl.ds(row0, blk), :], keys) + bias_ref[g, jnp.minimum(n, 1)]
            finish(g, dilation, r + dilation * n * blk, s, vals)
            return carry

        lax.fori_loop(0, dilation * nb, per_block, 0, unroll=ATTN_UNROLL)

    def combine(i, carry):
        rows = pl.ds(pl.multiple_of(i * blk, blk), blk)
        l0, l1, l2 = lse_sc[0, rows, :], lse_sc[1, rows, :], lse_sc[2, rows, :]
        m = jnp.maximum(jnp.maximum(l0, l1), l2)
        w0, w1, w2 = jnp.exp(l0 - m), jnp.exp(l1 - m), jnp.exp(l2 - m)
        o = (w0 * o_sc[0, rows, :] + w1 * o_sc[1, rows, :] + w2 * o_sc[2, rows, :])
        o = o * (1.0 / (w0 + w1 + w2))
        gate = gate_ref[rows, :].astype(F32)
        y_ref[rows, :] = (o * _silu(gate)).astype(y_ref.dtype)
        return carry

    lax.fori_loop(0, seq // blk, combine, 0)


def dilated_attention(planes, bias, batch, seq):
    hpg = HEADS_PER_GROUP

    def plane_spec(offset):
        return pl.BlockSpec((None, seq, LANES), lambda bi, h: (offset + h, bi, 0))

    in_specs, args = [], []
    for g in range(N_GROUPS_ATTN):
        for part in range(3):
            in_specs.append(plane_spec(part * hpg))
            args.append(planes[g])
    in_specs.append(plane_spec(3 * hpg))
    args.append(planes[0])
    in_specs.append(pl.BlockSpec((N_GROUPS_ATTN, None, 2, ATTN_BLOCK, 2 * ATTN_BLOCK),
                                 lambda bi, h: (0, h, 0, 0, 0)))
    args.append(bias)
    return pl.pallas_call(
        _attn_kernel,
        grid=(batch, hpg),
        in_specs=in_specs,
        out_specs=pl.BlockSpec((seq, LANES), lambda bi, h: (bi, h)),
        out_shape=jax.ShapeDtypeStruct((batch * seq, D_ATTN), BF16),
        scratch_shapes=[pltpu.VMEM((N_GROUPS_ATTN, seq, LANES), F32),
                        pltpu.VMEM((N_GROUPS_ATTN, seq, LANES), F32)],
        compiler_params=_cparams("parallel", "parallel"),
        name="dilated_attention",
    )(*args)


LN_ROWS = 32


def _outproj_ln_kernel(y_ref, w_ref, x_ref, g_ref, b_ref, *refs, alpha, nt, nk):
    o_refs, accs = refs[:-2], refs[-2:]
    i, k = pl.program_id(0), pl.program_id(1)
    rc = x_ref.shape[0]
    d_model = w_ref.shape[1]
    n_ln = rc // LN_ROWS
    n_mm = n_ln if d_model % (n_ln * 2 * LANES) == 0 else 1
    cols = d_model // n_mm

    def matmul_part(acc_ref, first, c):
        sl = slice(c * cols, (c + 1) * cols)
        d = jnp.dot(y_ref[...], w_ref[:, sl], preferred_element_type=F32)
        if first:
            acc_ref[:, sl] = d
        else:
            acc_ref[:, sl] += d

    def layer_norm_part(done_ref, j):
        sub = pl.ds(j * LN_ROWS, LN_ROWS)
        src = pl.ds(pl.multiple_of(k * rc + j * LN_ROWS, LN_ROWS), LN_ROWS)
        v = done_ref[src, :] + alpha * x_ref[sub, :]
        mu = jnp.mean(v, axis=-1, keepdims=True)
        c = v - mu
        var = jnp.mean(c * c, axis=-1, keepdims=True)
        out = c * lax.rsqrt(var + LN_EPS) * g_ref[...] + b_ref[...]
        o_refs[0][sub, :] = out
        if len(o_refs) > 1:
            o_refs[1][sub, :] = out.astype(BF16)

    for parity in (0, 1):
        acc_ref, done_ref = accs[parity], accs[1 - parity]
        mine = (i % 2 == parity)
        for first in (True, False):
            cond = mine & ((k == 0) if first else (k > 0))

            if parity == 0:
                @pl.when(cond & (i == 0))
                def _(acc_ref=acc_ref, first=first):
                    for c in range(n_mm):
                        matmul_part(acc_ref, first, c)

            @pl.when(cond & (i > 0) & (i < nt))
            def _(acc_ref=acc_ref, done_ref=done_ref, first=first):
                for step in range(max(n_ln, n_mm)):
                    if step < n_mm:
                        matmul_part(acc_ref, first, step)
                    if step < n_ln:
                        layer_norm_part(done_ref, step)

        if parity == nt % 2:
            @pl.when(i == nt)
            def _(done_ref=done_ref):
                for j in range(n_ln):
                    layer_norm_part(done_ref, j)


def outproj_layernorm(y, w, x_res, gamma, beta, alpha, bm, bk, emit_bf16):
    m, kdim = y.shape
    d_model = w.shape[1]
    nt, nk = m // bm, kdim // bk
    assert nk >= 2
    rc = bm // nk

    def chunk(i, k):
        return (jnp.where(i == 0, 0, (i - 1) * nk + k), 0)

    out_shape = [jax.ShapeDtypeStruct((m, d_model), F32)]
    out_specs = [pl.BlockSpec((rc, d_model), chunk)]
    if emit_bf16:
        out_shape.append(jax.ShapeDtypeStruct((m, d_model), BF16))
        out_specs.append(pl.BlockSpec((rc, d_model), chunk))
    return pl.pallas_call(
        functools.partial(_outproj_ln_kernel, alpha=alpha, nt=nt, nk=nk),
        grid=(nt + 1, nk),
        in_specs=[
            pl.BlockSpec((bm, bk), lambda i, k: (jnp.minimum(i, nt - 1), k)),
            pl.BlockSpec((bk, d_model), lambda i, k: (jnp.where(i == nt, 0, k), 0)),
            pl.BlockSpec((rc, d_model), chunk),
            pl.BlockSpec((1, d_model), lambda i, k: (0, 0)),
            pl.BlockSpec((1, d_model), lambda i, k: (0, 0)),
        ],
        out_specs=out_specs,
        out_shape=out_shape,
        scratch_shapes=[pltpu.VMEM((bm, d_model), F32), pltpu.VMEM((bm, d_model), F32)],
        compiler_params=_cparams("arbitrary", "arbitrary"),
        name="outproj_layernorm",
    )(y, w, x_res, gamma.reshape(1, d_model).astype(F32), beta.reshape(1, d_model).astype(F32))


def _dt_kernel(x_ref, w_ref, bias_ref, a_ref, dt_ref, da_ref):
    w = w_ref[...]
    w_hi = w.astype(BF16)
    w_lo = (w - w_hi.astype(F32)).astype(BF16)
    x = x_ref[...]
    raw = (jnp.dot(x, w_hi, preferred_element_type=F32)
           + jnp.dot(x, w_lo, preferred_element_type=F32))
    v = raw + bias_ref[...]
    dt = jnp.maximum(v, 0.0) + jnp.log1p(jnp.exp(-jnp.abs(v)))
    dt_ref[...] = dt.T
    da_ref[...] = (dt * a_ref[...]).T


def dt_projection(x, w_in, dt_bias, a_log, bm):
    m, d_model = x.shape
    nh = dt_bias.shape[0]
    dt_block = w_in.shape[1] // nh - 1
    a_row = (-jnp.exp(a_log.astype(F32))).reshape(1, nh)
    return pl.pallas_call(
        _dt_kernel,
        grid=(m // bm,),
        in_specs=[
            pl.BlockSpec((bm, d_model), lambda i: (i, 0)),
            pl.BlockSpec((d_model, nh), lambda i: (0, dt_block)),
            pl.BlockSpec((1, nh), lambda i: (0, 0)),
            pl.BlockSpec((1, nh), lambda i: (0, 0)),
        ],
        out_specs=[pl.BlockSpec((nh, bm), lambda i: (0, i)),
                   pl.BlockSpec((nh, bm), lambda i: (0, i))],
        out_shape=[jax.ShapeDtypeStruct((nh, m), F32), jax.ShapeDtypeStruct((nh, m), F32)],
        compiler_params=_cparams("parallel"),
        name="dt_projection",
    )(x, w_in, dt_bias.astype(F32).reshape(1, nh), a_row)


LOG2E = 1.4426950408889634
DT_FLOOR = 1e-37


def _split3(v):
    hi = v.astype(BF16)
    rest = v - hi.astype(F32)
    mid = rest.astype(BF16)
    lo = (rest - mid.astype(F32)).astype(BF16)
    return hi, mid, lo


def _ssd_kernel(xs_ref, b_ref, c_ref, z_ref, dt_ref, da_ref,
                wx_ref, wb_ref, wc_ref, cbx_ref, cbb_ref, cbc_ref,
                dskip_ref, normw_ref, expand_ref, o_ref,
                state_sc, carry_ref, *, nsub):
    L = CHUNK
    nh = HEADS_PER_SSM_GROUP
    hp = lax.Precision.HIGHEST
    nt_dims = (((1,), (1,)), ((), ()))

    @pl.when(pl.program_id(2) == 0)
    def _():
        state_sc[...] = jnp.zeros_like(state_sc)
        carry_ref[...] = jnp.zeros_like(carry_ref)

    row_i = lax.broadcasted_iota(jnp.int32, (L, L), 0)
    col_i = lax.broadcasted_iota(jnp.int32, (L, L), 1)
    causal = row_i >= col_i
    triu = (row_i <= col_i).astype(F32)
    eye = (row_i == col_i).astype(F32)
    eye_b = eye.astype(BF16)
    lane = lax.broadcasted_iota(jnp.int32, (L, LANES), 1)
    head_lo = (lane < SSM_HEAD_DIM).astype(F32).astype(BF16)
    head_hi = (lane >= SSM_HEAD_DIM).astype(F32).astype(BF16)

    shift_mat = jnp.concatenate(
        [(col_i == row_i - s).astype(F32).astype(BF16) for s in range(1, CONV_WIDTH)],
        axis=0)
    conv_w = jnp.concatenate([wx_ref[...], wb_ref[...], wc_ref[...]], axis=1)
    conv_bias = jnp.concatenate([cbx_ref[...], cbb_ref[...], cbc_ref[...]], axis=1)
    tail_rows = carry_ref.shape[0]
    tail_row = lax.broadcasted_iota(jnp.int32, carry_ref.shape, 0)

    def conv_silu(cur):
        cur_f = cur.astype(F32)
        tail = carry_ref[...]
        carry_ref[...] = cur_f[L - tail_rows:, :]
        shifted = jnp.dot(shift_mat, cur, preferred_element_type=F32)
        acc = conv_bias + conv_w[CONV_WIDTH - 1:CONV_WIDTH, :] * cur_f
        head = jnp.zeros_like(tail)
        for s in range(1, CONV_WIDTH):
            k = CONV_WIDTH - 1 - s
            acc = acc + conv_w[k:k + 1, :] * shifted[(s - 1) * L:s * L, :]
            head = head + conv_w[k:k + 1, :] * jnp.where(tail_row < s,
                                                          pltpu.roll(tail, s, 0), 0.0)
        acc = jnp.concatenate([acc[:tail_rows] + head, acc[tail_rows:]], axis=0)
        return _silu(acc)

    def per_chunk(j, carry):
        rows = pl.ds(pl.multiple_of(j * L, L), L)
        xbc = conv_silu(jnp.concatenate([xs_ref[rows, :], b_ref[rows, :], c_ref[rows, :]],
                                        axis=1))
        xs = xbc[:, :GROUP_CH]
        bm = xbc[:, GROUP_CH:GROUP_CH + D_STATE]
        cm = xbc[:, GROUP_CH + D_STATE:]
        xs_b, bm_b, cm_b = xs.astype(BF16), bm.astype(BF16), cm.astype(BF16)

        dt_t = dt_ref[:, rows]
        a_t = jnp.dot(da_ref[:, rows], triu, precision=hp,
                      preferred_element_type=F32) * LOG2E
        log2dt = jnp.log(jnp.maximum(dt_t, DT_FLOOR)) * LOG2E
        a_last = a_t[:, L - 1:L]
        r_t = a_t - log2dt
        e_t = jnp.exp2(a_t)
        w_t = jnp.exp2(a_last - r_t)
        a_cols = lax.dot_general(eye, a_t, nt_dims, precision=hp,
                                 preferred_element_type=F32)
        parts = _split3(jnp.concatenate([e_t, w_t], axis=0))
        parts = jnp.concatenate(parts + (jnp.zeros_like(parts[0]),), axis=0)
        cols3 = lax.dot_general(eye_b, parts, nt_dims,
                                preferred_element_type=F32).astype(BF16)
        ew = jnp.dot(cols3, expand_ref[...], preferred_element_type=F32)
        e_exp, w_exp = ew[:, :GROUP_CH], ew[:, GROUP_CH:]

        cb = lax.dot_general(cm_b, bm_b, nt_dims, preferred_element_type=F32)
        state = state_sc[...]
        z_off = jnp.dot(cm_b, state.astype(BF16), preferred_element_type=F32)

        ys = []
        for c in range(PAIRS):
            lanes = slice(c * LANES, (c + 1) * LANES)
            x_c = xs_b[:, lanes]
            rhs = jnp.concatenate([x_c * head_lo, x_c * head_hi], axis=0)
            lhs = []
            for h in (2 * c, 2 * c + 1):
                seg = a_cols[:, h:h + 1] - r_t[h:h + 1, :]
                lhs.append((jnp.exp2(jnp.where(causal, seg, -jnp.inf)) * cb).astype(BF16))
            y_c = jnp.dot(jnp.concatenate(lhs, axis=1), rhs, preferred_element_type=F32)
            ys.append(y_c + e_exp[:, lanes] * z_off[:, lanes])
        y = jnp.concatenate(ys, axis=1)

        upd = lax.dot_general(bm_b, (xs * w_exp).astype(BF16),
                              (((0,), (0,)), ((), ())), preferred_element_type=F32)
        state_sc[...] = state * e_exp[L - 1:L, :] + upd

        y = y + dskip_ref[...] * xs
        y = y * _silu(z_ref[rows, :].astype(F32))
        ms = jnp.mean(y * y, axis=-1, keepdims=True)
        y = y * lax.rsqrt(ms + RMS_EPS) * normw_ref[...]
        o_ref[rows, :] = y.astype(o_ref.dtype)
        return carry

    lax.fori_loop(0, nsub, per_chunk, 0, unroll=2)


def ssd_mixer_core(proj, dt_t, da_t, conv_w, conv_b, d_skip, norm_w, batch, seq, rows_per_step):
    m = proj.shape[0]
    d_inner = SSM_GROUPS * GROUP_CH
    t = rows_per_step
    nsteps = seq // t
    xcol0 = d_inner // GROUP_CH
    bcol0 = 2 * d_inner // D_STATE
    ccol0 = bcol0 + SSM_GROUPS
    wb0 = d_inner // D_STATE
    wc0 = wb0 + SSM_GROUPS
    conv_w = conv_w.astype(F32)
    conv_b = conv_b.astype(F32).reshape(1, -1)
    d_exp = jnp.repeat(d_skip.astype(F32), SSM_HEAD_DIM).reshape(1, d_inner)
    norm_w = norm_w.astype(F32).reshape(1, d_inner)
    head_exp = jnp.repeat(jnp.eye(HEADS_PER_SSM_GROUP, dtype=F32), SSM_HEAD_DIM, axis=1)
    zeros = jnp.zeros_like(head_exp)
    pair = jnp.block([[head_exp, zeros], [zeros, head_exp]])
    expand = jnp.concatenate([pair, pair, pair, jnp.zeros_like(pair)], axis=0).astype(BF16)

    def row(bi, s):
        return bi * nsteps + s

    in_specs = [
        pl.BlockSpec((t, GROUP_CH), lambda bi, g, s: (row(bi, s), xcol0 + g)),
        pl.BlockSpec((t, D_STATE), lambda bi, g, s: (row(bi, s), bcol0 + g)),
        pl.BlockSpec((t, D_STATE), lambda bi, g, s: (row(bi, s), ccol0 + g)),
        pl.BlockSpec((t, GROUP_CH), lambda bi, g, s: (row(bi, s), g)),
        pl.BlockSpec((HEADS_PER_SSM_GROUP, t), lambda bi, g, s: (g, row(bi, s))),
        pl.BlockSpec((HEADS_PER_SSM_GROUP, t), lambda bi, g, s: (g, row(bi, s))),
        pl.BlockSpec((CONV_WIDTH, GROUP_CH), lambda bi, g, s: (0, g)),
        pl.BlockSpec((CONV_WIDTH, D_STATE), lambda bi, g, s: (0, wb0 + g)),
        pl.BlockSpec((CONV_WIDTH, D_STATE), lambda bi, g, s: (0, wc0 + g)),
        pl.BlockSpec((1, GROUP_CH), lambda bi, g, s: (0, g)),
        pl.BlockSpec((1, D_STATE), lambda bi, g, s: (0, wb0 + g)),
        pl.BlockSpec((1, D_STATE), lambda bi, g, s: (0, wc0 + g)),
        pl.BlockSpec((1, GROUP_CH), lambda bi, g, s: (0, g)),
        pl.BlockSpec((1, GROUP_CH), lambda bi, g, s: (0, g)),
        pl.BlockSpec(expand.shape, lambda bi, g, s: (0, 0)),
    ]
    return pl.pallas_call(
        functools.partial(_ssd_kernel, nsub=t // CHUNK),
        grid=(batch, SSM_GROUPS, nsteps),
        in_specs=in_specs,
        out_specs=pl.BlockSpec((t, GROUP_CH), lambda bi, g, s: (row(bi, s), g)),
        out_shape=jax.ShapeDtypeStruct((m, d_inner), BF16),
        scratch_shapes=[pltpu.VMEM((D_STATE, GROUP_CH), F32),
                        pltpu.VMEM((CONV_TAIL_ROWS, GROUP_CH + 2 * D_STATE), F32)],
        compiler_params=_cparams("parallel", "parallel", "arbitrary"),
        name="ssd_mixer_core",
    )(proj, proj, proj, proj, dt_t, da_t, conv_w, conv_w, conv_w, conv_b, conv_b, conv_b,
      d_exp, norm_w, expand)


def _attention_layer(x, w_in, w_out, rel_bias, gamma, beta, alpha, emit_bf16):
    batch, seq, d_model = x.shape
    dilations = tuple(d for _, d in ATTN_PATTERNS)
    group_cols = 3 * D_ATTN
    gate_col0 = N_GROUPS_ATTN * group_cols
    wb = w_in.astype(BF16)
    xperm = cast_permute(x, dilations, bm=256)
    planes = []
    for g, (bm, bn) in enumerate(ATTN_PROJ_TILES):
        first = g * group_cols // bn
        if g == 0:
            nqkv, gate_first = group_cols // bn, gate_col0 // bn
            n = group_cols + D_ATTN
            col_block = lambda j, nqkv=nqkv, gate_first=gate_first: jnp.where(
                j < nqkv, j, j - nqkv + gate_first)
        else:
            n = group_cols
            col_block = lambda j, first=first: first + j
        planes.append(proj_planes(xperm[g], wb, col_block, n, D_ATTN, HEAD_DIM ** -0.5, bm, bn))
    y = dilated_attention(planes, masked_bias(rel_bias), batch, seq)
    return outproj_layernorm(y, w_out.astype(BF16), x.reshape(batch * seq, d_model),
                             gamma, beta, alpha, bm=512, bk=512, emit_bf16=emit_bf16)


def _ssd_layer(x, xb, w_in, conv_w, conv_b, dt_bias, a_log, d_skip, norm_w, w_out,
               gamma, beta, alpha):
    batch, seq, d_model = x.shape
    m = batch * seq
    n_main = w_in.shape[1] - dt_bias.shape[0]
    xf = x.reshape(m, d_model)
    proj = proj_flat(xb.reshape(m, d_model), w_in.astype(BF16), n_main, bm=1024, bn=1024)
    dt_t, da_t = dt_projection(xb.reshape(m, d_model), w_in, dt_bias, a_log, bm=1024)
    y = ssd_mixer_core(proj, dt_t, da_t, conv_w, conv_b, d_skip, norm_w, batch, seq,
                       rows_per_step=1024)
    return outproj_layernorm(y, w_out.astype(BF16), xf, gamma, beta, alpha,
                             bm=512, bk=1024, emit_bf16=False)


def kernel(x, w_in_attn, w_out_attn, rel_bias, w_in_ssm, conv_w, conv_b, dt_bias, a_log,
           d_skip, ssm_norm_w, w_out_ssm, ln_g, ln_b):
    batch, seq, d_model = x.shape
    depth = ln_g.shape[0]
    alpha = (2 * depth) ** 0.25
    xb = None
    for i in range(depth):
        j = i // 2
        if i % 2 == 0:
            outs = _attention_layer(x, w_in_attn[j], w_out_attn[j], rel_bias,
                                    ln_g[i], ln_b[i], alpha, emit_bf16=i + 1 < depth)
            if i + 1 < depth:
                xb = outs[1].reshape(batch, seq, d_model)
        else:
            outs = _ssd_layer(x, xb, w_in_ssm[j], conv_w[j], conv_b[j], dt_bias[j], a_log[j],
                              d_skip[j], ssm_norm_w[j], w_out_ssm[j], ln_g[i], ln_b[i], alpha)
        x = outs[0].reshape(batch, seq, d_model)
    return x
```

```python
import functools
import math

import jax
import jax.numpy as jnp
from jax import lax
from jax.experimental import pallas as pl
from jax.experimental.pallas import tpu as pltpu

F32 = jnp.float32
BF16 = jnp.bfloat16

LANES = 128
VMEM_LIMIT_BYTES = 56 * 1024 * 1024

ATTN_PATTERNS = ((128, 1), (512, 4), (2048, 16))
N_GROUPS_ATTN = 3
HEADS_PER_GROUP = 16
HEAD_DIM = 128
D_ATTN = HEADS_PER_GROUP * HEAD_DIM
ATTN_BLOCK = 128
NUM_BUCKETS = 32
MAX_DISTANCE = 2048
SSM_HEAD_DIM = 64
SSM_GROUPS = 8
HEADS_PER_SSM_GROUP = 16
D_STATE = 128
CONV_WIDTH = 4
CHUNK = 128
LN_EPS = 1e-5
RMS_EPS = 1e-5
NEG_INF = -1e30

GROUP_CH = HEADS_PER_SSM_GROUP * SSM_HEAD_DIM
PAIRS = GROUP_CH // LANES
CONV_TAIL_ROWS = 8


def _cparams(*sem):
    return pltpu.CompilerParams(dimension_semantics=sem, vmem_limit_bytes=VMEM_LIMIT_BYTES)


def _cast_permute_kernel(x_ref, *o_refs, dilations):
    xb = x_ref[...].astype(BF16)
    bm = xb.shape[0]
    dst = lax.broadcasted_iota(jnp.int32, (bm, bm), 0)
    src = lax.broadcasted_iota(jnp.int32, (bm, bm), 1)
    for o_ref, d in zip(o_refs, dilations):
        rows = bm // d
        if d == 1:
            o_ref[0] = xb
            continue
        perm = (src == (dst % rows) * d + dst // rows).astype(BF16)
        xp = jnp.dot(perm, xb, preferred_element_type=F32).astype(BF16)
        for r in range(d):
            o_ref[r] = xp[r * rows:(r + 1) * rows, :]


def cast_permute(x, dilations, bm):
    b, s, d_model = x.shape
    return pl.pallas_call(
        functools.partial(_cast_permute_kernel, dilations=dilations),
        grid=(b, s // bm),
        in_specs=[pl.BlockSpec((None, bm, d_model), lambda bi, i: (bi, i, 0))],
        out_specs=[pl.BlockSpec((None, d, bm // d, d_model), lambda bi, i: (bi, 0, i, 0))
                   for d in dilations],
        out_shape=[jax.ShapeDtypeStruct((b, d, s // d, d_model), BF16) for d in dilations],
        compiler_params=_cparams("parallel", "parallel"),
        name="cast_permute",
    )(x)


def _proj_kernel(x_ref, w_ref, o_ref, *, q_blocks, q_scale):
    x = x_ref[...].reshape(-1, x_ref.shape[-1])
    acc = jnp.dot(x, w_ref[...].astype(BF16), preferred_element_type=F32)
    if q_blocks:
        acc = acc * jnp.where(pl.program_id(1) < q_blocks, q_scale, 1.0)
    if len(o_ref.shape) == 3:
        for c in range(o_ref.shape[0]):
            o_ref[c] = acc[:, c * LANES:(c + 1) * LANES].astype(o_ref.dtype)
    else:
        o_ref[...] = acc.astype(o_ref.dtype)


def proj_planes(xp, w, col_block, n, q_cols, q_scale, bm, bn):
    b, dilation, sub_len, d_model = xp.shape
    bm = min(bm, dilation * sub_len)
    tiles = dilation * sub_len // bm
    if sub_len >= bm:
        per_class = sub_len // bm
        x_spec = pl.BlockSpec(
            (None, 1, bm, d_model),
            lambda t, j: (t // tiles, (t % tiles) // per_class, (t % tiles) % per_class, 0),
            pipeline_mode=pl.Buffered(1))
    else:
        x_spec = pl.BlockSpec((None, bm // sub_len, sub_len, d_model),
                              lambda t, j: (t // tiles, t % tiles, 0, 0),
                              pipeline_mode=pl.Buffered(1))
    return pl.pallas_call(
        functools.partial(_proj_kernel, q_blocks=q_cols // bn, q_scale=q_scale),
        grid=(b * tiles, n // bn),
        in_specs=[x_spec, pl.BlockSpec((d_model, bn), lambda t, j: (0, col_block(j)))],
        out_specs=pl.BlockSpec((bn // LANES, bm, LANES), lambda t, j: (j, t, 0)),
        out_shape=jax.ShapeDtypeStruct((n // LANES, b * dilation * sub_len, LANES), BF16),
        compiler_params=_cparams("parallel", "arbitrary"),
        name=f"proj_planes_d{dilation}",
    )(xp, w)


def proj_flat(xb, w, n_out, bm, bn):
    m, d_model = xb.shape
    bm = min(bm, m)
    return pl.pallas_call(
        functools.partial(_proj_kernel, q_blocks=0, q_scale=1.0),
        grid=(m // bm, n_out // bn),
        in_specs=[
            pl.BlockSpec((bm, d_model), lambda i, j: (i, 0), pipeline_mode=pl.Buffered(1)),
            pl.BlockSpec((d_model, bn), lambda i, j: (0, j)),
        ],
        out_specs=pl.BlockSpec((bm, bn), lambda i, j: (i, j)),
        out_shape=jax.ShapeDtypeStruct((m, n_out), BF16),
        compiler_params=_cparams("parallel", "arbitrary"),
        name="proj_flat",
    )(xb, w)


def _t5_causal_bucket(dist):
    max_exact = NUM_BUCKETS // 2
    d_f = jnp.maximum(dist, 1).astype(F32)
    large = max_exact + (jnp.log(d_f / max_exact) / math.log(MAX_DISTANCE / max_exact)
                         * (NUM_BUCKETS - max_exact)).astype(jnp.int32)
    large = jnp.minimum(large, NUM_BUCKETS - 1)
    return jnp.where(dist < max_exact, dist, large)


def _band_buckets():
    qi = jnp.arange(ATTN_BLOCK)[:, None]
    ki = jnp.arange(2 * ATTN_BLOCK)[None, :]
    delta = ATTN_BLOCK + qi - ki
    out = []
    for window, dilation in ATTN_PATTERNS:
        span = window // dilation
        band = (delta >= 0) & (delta <= span)
        bucket = _t5_causal_bucket(jnp.clip(delta, 0, None) * dilation)
        out.append(jnp.where(band, bucket, -1))
    return jnp.stack(out).astype(jnp.int32)


def _bias_kernel(table_ref, bucket_ref, o_ref):
    g = pl.program_id(0)
    bucket = bucket_ref[...]
    prev_half = lax.broadcasted_iota(jnp.int32, bucket.shape, 1) < ATTN_BLOCK

    def per_head(h, carry):
        acc = jnp.full(bucket.shape, NEG_INF, F32)
        for b in range(NUM_BUCKETS):
            acc = jnp.where(bucket == b, table_ref[b, g * HEADS_PER_GROUP + h], acc)
        o_ref[h, 0] = jnp.where(prev_half, NEG_INF, acc)
        o_ref[h, 1] = acc
        return carry

    lax.fori_loop(0, HEADS_PER_GROUP, per_head, 0)


def masked_bias(rel_bias):
    buckets = _band_buckets()
    return pl.pallas_call(
        _bias_kernel,
        grid=(N_GROUPS_ATTN,),
        in_specs=[
            pl.BlockSpec(memory_space=pltpu.SMEM),
            pl.BlockSpec((None, ATTN_BLOCK, 2 * ATTN_BLOCK), lambda g: (g, 0, 0)),
        ],
        out_specs=pl.BlockSpec((None, HEADS_PER_GROUP, 2, ATTN_BLOCK, 2 * ATTN_BLOCK),
                               lambda g: (g, 0, 0, 0, 0)),
        out_shape=jax.ShapeDtypeStruct(
            (N_GROUPS_ATTN, HEADS_PER_GROUP, 2, ATTN_BLOCK, 2 * ATTN_BLOCK), F32),
        compiler_params=_cparams("parallel"),
        name="masked_bias",
    )(rel_bias.astype(F32), buckets)


def _silu(v):
    h = 0.5 * v
    return h + h * jnp.tanh(h)


ATTN_UNROLL = 32
PROJ_TILE = (2048, 512)


def _attn_kernel(q0, k0, v0, q1, k1, v1, q2, k2, v2, gate_ref, bias_ref, y_ref, o_sc, lse_sc):
    seq = y_ref.shape[0]
    blk = ATTN_BLOCK
    qkv = ((q0, k0, v0), (q1, k1, v1), (q2, k2, v2))

    def finish(g, dilation, tok0, s, vals):
        m = jnp.max(s, axis=-1, keepdims=True)
        p = jnp.exp(s - m)
        denom = jnp.sum(p, axis=-1, keepdims=True)
        acc = jnp.dot(p.astype(BF16), vals, preferred_element_type=F32)
        o = acc * (1.0 / denom)
        lse = jnp.broadcast_to(m + jnp.log(denom), (blk, LANES))
        if dilation == 1:
            rows = pl.ds(tok0, blk)
        else:
            rows = pl.ds(tok0, blk, stride=dilation)
        o_sc[g, rows, :] = o
        lse_sc[g, rows, :] = lse

    def scores(q, keys):
        return lax.dot_general(q, keys, (((1,), (1,)), ((), ())), preferred_element_type=F32)

    for g, (_, dilation) in enumerate(ATTN_PATTERNS):
        q_ref, k_ref, v_ref = qkv[g]
        nb = seq // dilation // blk

        def per_block(idx, carry, g=g, dilation=dilation, q_ref=q_ref, k_ref=k_ref,
                      v_ref=v_ref, nb=nb):
            r, n = idx // nb, idx % nb
            row0 = pl.multiple_of(idx * blk, blk)
            prev0 = pl.multiple_of(jnp.maximum(row0 - blk, 0), blk)
            keys = jnp.concatenate([k_ref[pl.ds(prev0, blk), :],
                                    k_ref[pl.ds(row0, blk), :]], axis=0)
            vals = jnp.concatenate([v_ref[pl.ds(prev0, blk), :],
                                    v_ref[pl.ds(row0, blk), :]], axis=0)
            s = scores(q_ref[pl.ds(row0, blk), :], keys) + bias_ref[g, jnp.minimum(n, 1)]
            finish(g, dilation, r + dilation * n * blk, s, vals)
            return carry

        lax.fori_loop(0, dilation * nb, per_block, 0, unroll=ATTN_UNROLL)

    def combine(i, carry):
        rows = pl.ds(pl.multiple_of(i * blk, blk), blk)
        l0, l1, l2 = lse_sc[0, rows, :], lse_sc[1, rows, :], lse_sc[2, rows, :]
        m = jnp.maximum(jnp.maximum(l0, l1), l2)
        w0, w1, w2 = jnp.exp(l0 - m), jnp.exp(l1 - m), jnp.exp(l2 - m)
        o = (w0 * o_sc[0, rows, :] + w1 * o_sc[1, rows, :] + w2 * o_sc[2, rows, :])
        o = o * (1.0 / (w0 + w1 + w2))
        gate = gate_ref[rows, :].astype(F32)
        y_ref[rows, :] = (o * _silu(gate)).astype(y_ref.dtype)
        return carry

    lax.fori_loop(0, seq // blk, combine, 0)


def dilated_attention(planes, bias, batch, seq):
    hpg = HEADS_PER_GROUP

    def plane_spec(offset):
        return pl.BlockSpec((None, seq, LANES), lambda bi, h: (offset + h, bi, 0))

    in_specs, args = [], []
    for g in range(N_GROUPS_ATTN):
        for part in range(3):
            in_specs.append(plane_spec(part * hpg))
            args.append(planes[g])
    in_specs.append(plane_spec(3 * hpg))
    args.append(planes[0])
    in_specs.append(pl.BlockSpec((N_GROUPS_ATTN, None, 2, ATTN_BLOCK, 2 * ATTN_BLOCK),
                                 lambda bi, h: (0, h, 0, 0, 0)))
    args.append(bias)
    return pl.pallas_call(
        _attn_kernel,
        grid=(batch, hpg),
        in_specs=in_specs,
        out_specs=pl.BlockSpec((seq, LANES), lambda bi, h: (bi, h)),
        out_shape=jax.ShapeDtypeStruct((batch * seq, D_ATTN), BF16),
        scratch_shapes=[pltpu.VMEM((N_GROUPS_ATTN, seq, LANES), F32),
                        pltpu.VMEM((N_GROUPS_ATTN, seq, LANES), F32)],
        compiler_params=_cparams("parallel", "parallel"),
        name="dilated_attention",
    )(*args)


LN_ROWS = 32


def _outproj_ln_kernel(y_ref, w_ref, x_ref, g_ref, b_ref, *refs, alpha, nt, nk):
    o_refs, accs = refs[:-2], refs[-2:]
    i, k = pl.program_id(0), pl.program_id(1)
    rc = x_ref.shape[0]
    d_model = w_ref.shape[1]
    n_ln = rc // LN_ROWS
    n_mm = n_ln if d_model % (n_ln * 2 * LANES) == 0 else 1
    cols = d_model // n_mm

    def matmul_part(acc_ref, first, c):
        sl = slice(c * cols, (c + 1) * cols)
        d = jnp.dot(y_ref[...], w_ref[:, sl], preferred_element_type=F32)
        if first:
            acc_ref[:, sl] = d
        else:
            acc_ref[:, sl] += d

    def layer_norm_part(done_ref, j):
        sub = pl.ds(j * LN_ROWS, LN_ROWS)
        src = pl.ds(pl.multiple_of(k * rc + j * LN_ROWS, LN_ROWS), LN_ROWS)
        v = done_ref[src, :] + alpha * x_ref[sub, :]
        mu = jnp.mean(v, axis=-1, keepdims=True)
        c = v - mu
        var = jnp.mean(c * c, axis=-1, keepdims=True)
        out = c * lax.rsqrt(var + LN_EPS) * g_ref[...] + b_ref[...]
        o_refs[0][sub, :] = out
        if len(o_refs) > 1:
            o_refs[1][sub, :] = out.astype(BF16)

    for parity in (0, 1):
        acc_ref, done_ref = accs[parity], accs[1 - parity]
        mine = (i % 2 == parity)
        for first in (True, False):
            cond = mine & ((k == 0) if first else (k > 0))

            if parity == 0:
                @pl.when(cond & (i == 0))
                def _(acc_ref=acc_ref, first=first):
                    for c in range(n_mm):
                        matmul_part(acc_ref, first, c)

            @pl.when(cond & (i > 0) & (i < nt))
            def _(acc_ref=acc_ref, done_ref=done_ref, first=first):
                for step in range(max(n_ln, n_mm)):
                    if step < n_mm:
                        matmul_part(acc_ref, first, step)
                    if step < n_ln:
                        layer_norm_part(done_ref, step)

        if parity == nt % 2:
            @pl.when(i == nt)
            def _(done_ref=done_ref):
                for j in range(n_ln):
                    layer_norm_part(done_ref, j)


def outproj_layernorm(y, w, x_res, gamma, beta, alpha, bm, bk, emit_bf16):
    m, kdim = y.shape
    d_model = w.shape[1]
    nt, nk = m // bm, kdim // bk
    assert nk >= 2
    rc = bm // nk

    def chunk(i, k):
        return (jnp.where(i == 0, 0, (i - 1) * nk + k), 0)

    out_shape = [jax.ShapeDtypeStruct((m, d_model), F32)]
    out_specs = [pl.BlockSpec((rc, d_model), chunk)]
    if emit_bf16:
        out_shape.append(jax.ShapeDtypeStruct((m, d_model), BF16))
        out_specs.append(pl.BlockSpec((rc, d_model), chunk))
    return pl.pallas_call(
        functools.partial(_outproj_ln_kernel, alpha=alpha, nt=nt, nk=nk),
        grid=(nt + 1, nk),
        in_specs=[
            pl.BlockSpec((bm, bk), lambda i, k: (jnp.minimum(i, nt - 1), k)),
            pl.BlockSpec((bk, d_model), lambda i, k: (jnp.where(i == nt, 0, k), 0)),
            pl.BlockSpec((rc, d_model), chunk),
            pl.BlockSpec((1, d_model), lambda i, k: (0, 0)),
            pl.BlockSpec((1, d_model), lambda i, k: (0, 0)),
        ],
        out_specs=out_specs,
        out_shape=out_shape,
        scratch_shapes=[pltpu.VMEM((bm, d_model), F32), pltpu.VMEM((bm, d_model), F32)],
        compiler_params=_cparams("arbitrary", "arbitrary"),
        name="outproj_layernorm",
    )(y, w, x_res, gamma.reshape(1, d_model).astype(F32), beta.reshape(1, d_model).astype(F32))


def _dt_kernel(x_ref, w_ref, bias_ref, a_ref, dt_ref, da_ref):
    w = w_ref[...]
    w_hi = w.astype(BF16)
    w_lo = (w - w_hi.astype(F32)).astype(BF16)
    x = x_ref[...]
    raw = (jnp.dot(x, w_hi, preferred_element_type=F32)
           + jnp.dot(x, w_lo, preferred_element_type=F32))
    v = raw + bias_ref[...]
    dt = jnp.maximum(v, 0.0) + jnp.log1p(jnp.exp(-jnp.abs(v)))
    dt_ref[...] = dt.T
    da_ref[...] = (dt * a_ref[...]).T


def dt_projection(x, w_in, dt_bias, a_log, bm):
    m, d_model = x.shape
    nh = dt_bias.shape[0]
    dt_block = w_in.shape[1] // nh - 1
    a_row = (-jnp.exp(a_log.astype(F32))).reshape(1, nh)
    return pl.pallas_call(
        _dt_kernel,
        grid=(m // bm,),
        in_specs=[
            pl.BlockSpec((bm, d_model), lambda i: (i, 0)),
            pl.BlockSpec((d_model, nh), lambda i: (0, dt_block)),
            pl.BlockSpec((1, nh), lambda i: (0, 0)),
            pl.BlockSpec((1, nh), lambda i: (0, 0)),
        ],
        out_specs=[pl.BlockSpec((nh, bm), lambda i: (0, i)),
                   pl.BlockSpec((nh, bm), lambda i: (0, i))],
        out_shape=[jax.ShapeDtypeStruct((nh, m), F32), jax.ShapeDtypeStruct((nh, m), F32)],
        compiler_params=_cparams("parallel"),
        name="dt_projection",
    )(x, w_in, dt_bias.astype(F32).reshape(1, nh), a_row)


LOG2E = 1.4426950408889634
DT_FLOOR = 1e-37


def _split3(v):
    hi = v.astype(BF16)
    rest = v - hi.astype(F32)
    mid = rest.astype(BF16)
    lo = (rest - mid.astype(F32)).astype(BF16)
    return hi, mid, lo


def _ssd_kernel(xs_ref, b_ref, c_ref, z_ref, dt_ref, da_ref,
                wx_ref, wb_ref, wc_ref, cbx_ref, cbb_ref, cbc_ref,
                dskip_ref, normw_ref, expand_ref, o_ref,
                state_sc, carry_ref, *, nsub):
    L = CHUNK
    nh = HEADS_PER_SSM_GROUP
    hp = lax.Precision.HIGHEST
    nt_dims = (((1,), (1,)), ((), ()))

    @pl.when(pl.program_id(2) == 0)
    def _():
        state_sc[...] = jnp.zeros_like(state_sc)
        carry_ref[...] = jnp.zeros_like(carry_ref)

    row_i = lax.broadcasted_iota(jnp.int32, (L, L), 0)
    col_i = lax.broadcasted_iota(jnp.int32, (L, L), 1)
    causal = row_i >= col_i
    triu = (row_i <= col_i).astype(F32)
    eye = (row_i == col_i).astype(F32)
    eye_b = eye.astype(BF16)
    lane = lax.broadcasted_iota(jnp.int32, (L, LANES), 1)
    head_lo = (lane < SSM_HEAD_DIM).astype(F32).astype(BF16)
    head_hi = (lane >= SSM_HEAD_DIM).astype(F32).astype(BF16)

    shift_mat = jnp.concatenate(
        [(col_i == row_i - s).astype(F32).astype(BF16) for s in range(1, CONV_WIDTH)],
        axis=0)
    conv_w = jnp.concatenate([wx_ref[...], wb_ref[...], wc_ref[...]], axis=1)
    conv_bias = jnp.concatenate([cbx_ref[...], cbb_ref[...], cbc_ref[...]], axis=1)
    tail_rows = carry_ref.shape[0]
    tail_row = lax.broadcasted_iota(jnp.int32, carry_ref.shape, 0)

    def conv_silu(cur):
        cur_f = cur.astype(F32)
        tail = carry_ref[...]
        carry_ref[...] = cur_f[L - tail_rows:, :]
        shifted = jnp.dot(shift_mat, cur, preferred_element_type=F32)
        acc = conv_bias + conv_w[CONV_WIDTH - 1:CONV_WIDTH, :] * cur_f
        head = jnp.zeros_like(tail)
        for s in range(1, CONV_WIDTH):
            k = CONV_WIDTH - 1 - s
            acc = acc + conv_w[k:k + 1, :] * shifted[(s - 1) * L:s * L, :]
            head = head + conv_w[k:k + 1, :] * jnp.where(tail_row < s,
                                                          pltpu.roll(tail, s, 0), 0.0)
        acc = jnp.concatenate([acc[:tail_rows] + head, acc[tail_rows:]], axis=0)
        return _silu(acc)

    def per_chunk(j, carry):
        rows = pl.ds(pl.multiple_of(j * L, L), L)
        xbc = conv_silu(jnp.concatenate([xs_ref[rows, :], b_ref[rows, :], c_ref[rows, :]],
                                        axis=1))
        xs = xbc[:, :GROUP_CH]
        bm = xbc[:, GROUP_CH:GROUP_CH + D_STATE]
        cm = xbc[:, GROUP_CH + D_STATE:]
        xs_b, bm_b, cm_b = xs.astype(BF16), bm.astype(BF16), cm.astype(BF16)

        dt_t = dt_ref[:, rows]
        a_t = jnp.dot(da_ref[:, rows], triu, precision=hp,
                      preferred_element_type=F32) * LOG2E
        log2dt = jnp.log(jnp.maximum(dt_t, DT_FLOOR)) * LOG2E
        a_last = a_t[:, L - 1:L]
        r_t = a_t - log2dt
        e_t = jnp.exp2(a_t)
        w_t = jnp.exp2(a_last - r_t)
        a_cols = lax.dot_general(eye, a_t, nt_dims, precision=hp,
                                 preferred_element_type=F32)
        parts = _split3(jnp.concatenate([e_t, w_t], axis=0))
        parts = jnp.concatenate(parts + (jnp.zeros_like(parts[0]),), axis=0)
        cols3 = lax.dot_general(eye_b, parts, nt_dims,
                                preferred_element_type=F32).astype(BF16)
        ew = jnp.dot(cols3, expand_ref[...], preferred_element_type=F32)
        e_exp, w_exp = ew[:, :GROUP_CH], ew[:, GROUP_CH:]

        cb = lax.dot_general(cm_b, bm_b, nt_dims, preferred_element_type=F32)
        state = state_sc[...]
        z_off = jnp.dot(cm_b, state.astype(BF16), preferred_element_type=F32)

        ys = []
        for c in range(PAIRS):
            lanes = slice(c * LANES, (c + 1) * LANES)
            x_c = xs_b[:, lanes]
            rhs = jnp.concatenate([x_c * head_lo, x_c * head_hi], axis=0)
            lhs = []
            for h in (2 * c, 2 * c + 1):
                seg = a_cols[:, h:h + 1] - r_t[h:h + 1, :]
                lhs.append((jnp.exp2(jnp.where(causal, seg, -jnp.inf)) * cb).astype(BF16))
            y_c = jnp.dot(jnp.concatenate(lhs, axis=1), rhs, preferred_element_type=F32)
            ys.append(y_c + e_exp[:, lanes] * z_off[:, lanes])
        y = jnp.concatenate(ys, axis=1)

        upd = lax.dot_general(bm_b, (xs * w_exp).astype(BF16),
                              (((0,), (0,)), ((), ())), preferred_element_type=F32)
        state_sc[...] = state * e_exp[L - 1:L, :] + upd

        y = y + dskip_ref[...] * xs
        y = y * _silu(z_ref[rows, :].astype(F32))
        ms = jnp.mean(y * y, axis=-1, keepdims=True)
        y = y * lax.rsqrt(ms + RMS_EPS) * normw_ref[...]
        o_ref[rows, :] = y.astype(o_ref.dtype)
        return carry

    lax.fori_loop(0, nsub, per_chunk, 0, unroll=2)


def ssd_mixer_core(proj, dt_t, da_t, conv_w, conv_b, d_skip, norm_w, batch, seq, rows_per_step):
    m = proj.shape[0]
    d_inner = SSM_GROUPS * GROUP_CH
    t = rows_per_step
    nsteps = seq // t
    xcol0 = d_inner // GROUP_CH
    bcol0 = 2 * d_inner // D_STATE
    ccol0 = bcol0 + SSM_GROUPS
    wb0 = d_inner // D_STATE
    wc0 = wb0 + SSM_GROUPS
    conv_w = conv_w.astype(F32)
    conv_b = conv_b.astype(F32).reshape(1, -1)
    d_exp = jnp.repeat(d_skip.astype(F32), SSM_HEAD_DIM).reshape(1, d_inner)
    norm_w = norm_w.astype(F32).reshape(1, d_inner)
    head_exp = jnp.repeat(jnp.eye(HEADS_PER_SSM_GROUP, dtype=F32), SSM_HEAD_DIM, axis=1)
    zeros = jnp.zeros_like(head_exp)
    pair = jnp.block([[head_exp, zeros], [zeros, head_exp]])
    expand = jnp.concatenate([pair, pair, pair, jnp.zeros_like(pair)], axis=0).astype(BF16)

    def row(bi, s):
        return bi * nsteps + s

    in_specs = [
        pl.BlockSpec((t, GROUP_CH), lambda bi, g, s: (row(bi, s), xcol0 + g)),
        pl.BlockSpec((t, D_STATE), lambda bi, g, s: (row(bi, s), bcol0 + g)),
        pl.BlockSpec((t, D_STATE), lambda bi, g, s: (row(bi, s), ccol0 + g)),
        pl.BlockSpec((t, GROUP_CH), lambda bi, g, s: (row(bi, s), g)),
        pl.BlockSpec((HEADS_PER_SSM_GROUP, t), lambda bi, g, s: (g, row(bi, s))),
        pl.BlockSpec((HEADS_PER_SSM_GROUP, t), lambda bi, g, s: (g, row(bi, s))),
        pl.BlockSpec((CONV_WIDTH, GROUP_CH), lambda bi, g, s: (0, g)),
        pl.BlockSpec((CONV_WIDTH, D_STATE), lambda bi, g, s: (0, wb0 + g)),
        pl.BlockSpec((CONV_WIDTH, D_STATE), lambda bi, g, s: (0, wc0 + g)),
        pl.BlockSpec((1, GROUP_CH), lambda bi, g, s: (0, g)),
        pl.BlockSpec((1, D_STATE), lambda bi, g, s: (0, wb0 + g)),
        pl.BlockSpec((1, D_STATE), lambda bi, g, s: (0, wc0 + g)),
        pl.BlockSpec((1, GROUP_CH), lambda bi, g, s: (0, g)),
        pl.BlockSpec((1, GROUP_CH), lambda bi, g, s: (0, g)),
        pl.BlockSpec(expand.shape, lambda bi, g, s: (0, 0)),
    ]
    return pl.pallas_call(
        functools.partial(_ssd_kernel, nsub=t // CHUNK),
        grid=(batch, SSM_GROUPS, nsteps),
        in_specs=in_specs,
        out_specs=pl.BlockSpec((t, GROUP_CH), lambda bi, g, s: (row(bi, s), g)),
        out_shape=jax.ShapeDtypeStruct((m, d_inner), BF16),
        scratch_shapes=[pltpu.VMEM((D_STATE, GROUP_CH), F32),
                        pltpu.VMEM((CONV_TAIL_ROWS, GROUP_CH + 2 * D_STATE), F32)],
        compiler_params=_cparams("parallel", "parallel", "arbitrary"),
        name="ssd_mixer_core",
    )(proj, proj, proj, proj, dt_t, da_t, conv_w, conv_w, conv_w, conv_b, conv_b, conv_b,
      d_exp, norm_w, expand)


def _attention_layer(x, w_in, w_out, rel_bias, gamma, beta, alpha, emit_bf16):
    batch, seq, d_model = x.shape
    dilations = tuple(d for _, d in ATTN_PATTERNS)
    group_cols = 3 * D_ATTN
    gate_col0 = N_GROUPS_ATTN * group_cols
    xperm = cast_permute(x, dilations, bm=256)
    planes = []
    bm, bn = PROJ_TILE
    for g in range(N_GROUPS_ATTN):
        first = g * group_cols // bn
        if g == 0:
            nqkv, gate_first = group_cols // bn, gate_col0 // bn
            n = group_cols + D_ATTN
            col_block = lambda j, nqkv=nqkv, gate_first=gate_first: jnp.where(
                j < nqkv, j, j - nqkv + gate_first)
        else:
            n = group_cols
            col_block = lambda j, first=first: first + j
        planes.append(proj_planes(xperm[g], w_in.astype(F32), col_block, n, D_ATTN,
                                  HEAD_DIM ** -0.5, bm, bn))
    y = dilated_attention(planes, masked_bias(rel_bias), batch, seq)
    return outproj_layernorm(y, w_out.astype(BF16), x.reshape(batch * seq, d_model),
                             gamma, beta, alpha, bm=512, bk=512, emit_bf16=emit_bf16)


def _ssd_layer(x, xb, w_in, conv_w, conv_b, dt_bias, a_log, d_skip, norm_w, w_out,
               gamma, beta, alpha):
    batch, seq, d_model = x.shape
    m = batch * seq
    n_main = w_in.shape[1] - dt_bias.shape[0]
    xf = x.reshape(m, d_model)
    proj = proj_flat(xb.reshape(m, d_model), w_in.astype(F32), n_main, *PROJ_TILE)
    dt_t, da_t = dt_projection(xb.reshape(m, d_model), w_in, dt_bias, a_log, bm=1024)
    y = ssd_mixer_core(proj, dt_t, da_t, conv_w, conv_b, d_skip, norm_w, batch, seq,
                       rows_per_step=1024)
    return outproj_layernorm(y, w_out.astype(BF16), xf, gamma, beta, alpha,
                             bm=512, bk=1024, emit_bf16=False)


def kernel(x, w_in_attn, w_out_attn, rel_bias, w_in_ssm, conv_w, conv_b, dt_bias, a_log,
           d_skip, ssm_norm_w, w_out_ssm, ln_g, ln_b):
    batch, seq, d_model = x.shape
    depth = ln_g.shape[0]
    alpha = (2 * depth) ** 0.25
    xb = None
    for i in range(depth):
        j = i // 2
        if i % 2 == 0:
            outs = _attention_layer(x, w_in_attn[j], w_out_attn[j], rel_bias,
                                    ln_g[i], ln_b[i], alpha, emit_bf16=i + 1 < depth)
            if i + 1 < depth:
                xb = outs[1].reshape(batch, seq, d_model)
        else:
            outs = _ssd_layer(x, xb, w_in_ssm[j], conv_w[j], conv_b[j], dt_bias[j], a_log[j],
                              d_skip[j], ssm_norm_w[j], w_out_ssm[j], ln_g[i], ln_b[i], alpha)
        x = outs[0].reshape(batch, seq, d_model)
    return x
```

```python
import functools
import math

import jax
import jax.numpy as jnp
from jax import lax
from jax.experimental import pallas as pl
from jax.experimental.pallas import tpu as pltpu

F32 = jnp.float32
BF16 = jnp.bfloat16

LANES = 128
VMEM_LIMIT_BYTES = 56 * 1024 * 1024

ATTN_PATTERNS = ((128, 1), (512, 4), (2048, 16))
N_GROUPS_ATTN = 3
HEADS_PER_GROUP = 16
HEAD_DIM = 128
D_ATTN = HEADS_PER_GROUP * HEAD_DIM
ATTN_BLOCK = 128
NUM_BUCKETS = 32
MAX_DISTANCE = 2048
SSM_HEAD_DIM = 64
SSM_GROUPS = 8
HEADS_PER_SSM_GROUP = 16
D_STATE = 128
CONV_WIDTH = 4
CHUNK = 128
LN_EPS = 1e-5
RMS_EPS = 1e-5
NEG_INF = -1e30
LOG2E = 1.4426950408889634

CAST_ROWS = 256
PROJ_TILE = (2048, 512)
ATTN_UNROLL = 32
OUTPROJ_TILE_ATTN = (512, 512)
OUTPROJ_TILE_SSM = (1024, 512)
LN_ROWS = 32
DT_ROWS = 1024
SSD_ROWS = 2048
SSD_CHUNK_UNROLL = 2

GROUP_CH = HEADS_PER_SSM_GROUP * SSM_HEAD_DIM
PAIRS = GROUP_CH // LANES
CONV_TAIL_ROWS = 8


def _cparams(*sem):
    return pltpu.CompilerParams(dimension_semantics=sem, vmem_limit_bytes=VMEM_LIMIT_BYTES)


def _cast_permute_kernel(x_ref, *o_refs, dilations):
    xb = x_ref[...].astype(BF16)
    bm = xb.shape[0]
    dst = lax.broadcasted_iota(jnp.int32, (bm, bm), 0)
    src = lax.broadcasted_iota(jnp.int32, (bm, bm), 1)
    for o_ref, d in zip(o_refs, dilations):
        rows = bm // d
        if d == 1:
            o_ref[0] = xb
            continue
        perm = (src == (dst % rows) * d + dst // rows).astype(BF16)
        xp = jnp.dot(perm, xb, preferred_element_type=F32).astype(BF16)
        for r in range(d):
            o_ref[r] = xp[r * rows:(r + 1) * rows, :]


def cast_permute(x, dilations, bm):
    b, s, d_model = x.shape
    return pl.pallas_call(
        functools.partial(_cast_permute_kernel, dilations=dilations),
        grid=(b, s // bm),
        in_specs=[pl.BlockSpec((None, bm, d_model), lambda bi, i: (bi, i, 0))],
        out_specs=[pl.BlockSpec((None, d, bm // d, d_model), lambda bi, i: (bi, 0, i, 0))
                   for d in dilations],
        out_shape=[jax.ShapeDtypeStruct((b, d, s // d, d_model), BF16) for d in dilations],
        compiler_params=_cparams("parallel", "parallel"),
        name="cast_permute",
    )(x)


def _silu(v):
    h = 0.5 * v
    return h + h * jnp.tanh(h)


def _proj_kernel(x_ref, w_ref, o_ref, *, q_blocks, q_scale):
    x = x_ref[...].reshape(-1, x_ref.shape[-1])
    acc = jnp.dot(x, w_ref[...].astype(BF16), preferred_element_type=F32)
    if q_blocks:
        acc = acc * jnp.where(pl.program_id(1) < q_blocks, q_scale, 1.0)
    if len(o_ref.shape) == 3:
        for c in range(o_ref.shape[0]):
            o_ref[c] = acc[:, c * LANES:(c + 1) * LANES].astype(o_ref.dtype)
    else:
        o_ref[...] = acc.astype(o_ref.dtype)


def proj_planes(xp, w, col_block, n, q_cols, q_scale, bm, bn):
    b, dilation, sub_len, d_model = xp.shape
    bm = min(bm, dilation * sub_len)
    tiles = dilation * sub_len // bm
    if sub_len >= bm:
        per_class = sub_len // bm
        x_spec = pl.BlockSpec(
            (None, 1, bm, d_model),
            lambda t, j: (t // tiles, (t % tiles) // per_class, (t % tiles) % per_class, 0),
            pipeline_mode=pl.Buffered(1))
    else:
        x_spec = pl.BlockSpec((None, bm // sub_len, sub_len, d_model),
                              lambda t, j: (t // tiles, t % tiles, 0, 0),
                              pipeline_mode=pl.Buffered(1))
    return pl.pallas_call(
        functools.partial(_proj_kernel, q_blocks=q_cols // bn, q_scale=q_scale),
        grid=(b * tiles, n // bn),
        in_specs=[x_spec, pl.BlockSpec((d_model, bn), lambda t, j: (0, col_block(j)))],
        out_specs=pl.BlockSpec((bn // LANES, bm, LANES), lambda t, j: (j, t, 0)),
        out_shape=jax.ShapeDtypeStruct((n // LANES, b * dilation * sub_len, LANES), BF16),
        compiler_params=_cparams("parallel", "arbitrary"),
        name=f"proj_planes_d{dilation}",
    )(xp, w)


def proj_flat(xb, w, n_out, bm, bn):
    m, d_model = xb.shape
    bm = min(bm, m)
    return pl.pallas_call(
        functools.partial(_proj_kernel, q_blocks=0, q_scale=1.0),
        grid=(m // bm, n_out // bn),
        in_specs=[
            pl.BlockSpec((bm, d_model), lambda i, j: (i, 0), pipeline_mode=pl.Buffered(1)),
            pl.BlockSpec((d_model, bn), lambda i, j: (0, j)),
        ],
        out_specs=pl.BlockSpec((bm, bn), lambda i, j: (i, j)),
        out_shape=jax.ShapeDtypeStruct((m, n_out), BF16),
        compiler_params=_cparams("parallel", "arbitrary"),
        name="proj_flat",
    )(xb, w)


def _t5_causal_bucket(dist):
    max_exact = NUM_BUCKETS // 2
    d_f = jnp.maximum(dist, 1).astype(F32)
    large = max_exact + (jnp.log(d_f / max_exact) / math.log(MAX_DISTANCE / max_exact)
                         * (NUM_BUCKETS - max_exact)).astype(jnp.int32)
    large = jnp.minimum(large, NUM_BUCKETS - 1)
    return jnp.where(dist < max_exact, dist, large)


def _band_buckets():
    qi = jnp.arange(ATTN_BLOCK)[:, None]
    ki = jnp.arange(2 * ATTN_BLOCK)[None, :]
    delta = ATTN_BLOCK + qi - ki
    out = []
    for window, dilation in ATTN_PATTERNS:
        span = window // dilation
        band = (delta >= 0) & (delta <= span)
        bucket = _t5_causal_bucket(jnp.clip(delta, 0, None) * dilation)
        out.append(jnp.where(band, bucket, -1))
    return jnp.stack(out).astype(jnp.int32)


def _bias_kernel(table_ref, bucket_ref, o_ref):
    g = pl.program_id(0)
    bucket = bucket_ref[...]
    prev_half = lax.broadcasted_iota(jnp.int32, bucket.shape, 1) < ATTN_BLOCK

    def per_head(h, carry):
        acc = jnp.full(bucket.shape, NEG_INF, F32)
        for b in range(NUM_BUCKETS):
            acc = jnp.where(bucket == b, table_ref[b, g * HEADS_PER_GROUP + h] * LOG2E, acc)
        o_ref[h, 0] = jnp.where(prev_half, NEG_INF, acc)
        o_ref[h, 1] = acc
        return carry

    lax.fori_loop(0, HEADS_PER_GROUP, per_head, 0)


def masked_bias(rel_bias):
    buckets = _band_buckets()
    return pl.pallas_call(
        _bias_kernel,
        grid=(N_GROUPS_ATTN,),
        in_specs=[
            pl.BlockSpec(memory_space=pltpu.SMEM),
            pl.BlockSpec((None, ATTN_BLOCK, 2 * ATTN_BLOCK), lambda g: (g, 0, 0)),
        ],
        out_specs=pl.BlockSpec((None, HEADS_PER_GROUP, 2, ATTN_BLOCK, 2 * ATTN_BLOCK),
                               lambda g: (g, 0, 0, 0, 0)),
        out_shape=jax.ShapeDtypeStruct(
            (N_GROUPS_ATTN, HEADS_PER_GROUP, 2, ATTN_BLOCK, 2 * ATTN_BLOCK), F32),
        compiler_params=_cparams("parallel"),
        name="masked_bias",
    )(rel_bias.astype(F32), buckets)


def _attn_kernel(q0, k0, v0, q1, k1, v1, q2, k2, v2, gate_ref, bias_ref, y_ref, o_sc, lse_sc):
    seq = y_ref.shape[0]
    blk = ATTN_BLOCK
    qkv = ((q0, k0, v0), (q1, k1, v1), (q2, k2, v2))

    def finish(g, dilation, tok0, s, vals):
        m = jnp.max(s, axis=-1, keepdims=True)
        p = jnp.exp2(s - m)
        denom = jnp.sum(p, axis=-1, keepdims=True)
        acc = jnp.dot(p.astype(BF16), vals, preferred_element_type=F32)
        o = acc * (1.0 / denom)
        lse = jnp.broadcast_to(m + jnp.log2(denom), (blk, LANES))
        if dilation > 1:
            rows = pl.ds(tok0, blk, stride=dilation)
            o_sc[g - 1, rows, :] = o
            lse_sc[g - 1, rows, :] = lse
            return
        rows = pl.ds(tok0, blk)
        l1, l2 = lse_sc[0, rows, :], lse_sc[1, rows, :]
        top = jnp.maximum(jnp.maximum(lse, l1), l2)
        w0, w1, w2 = jnp.exp2(lse - top), jnp.exp2(l1 - top), jnp.exp2(l2 - top)
        o = (w0 * o + w1 * o_sc[0, rows, :] + w2 * o_sc[1, rows, :]) * (1.0 / (w0 + w1 + w2))
        gate = gate_ref[rows, :].astype(F32)
        y_ref[rows, :] = (o * _silu(gate)).astype(y_ref.dtype)

    def scores(q, keys):
        return lax.dot_general(q, keys, (((1,), (1,)), ((), ())), preferred_element_type=F32)

    assert ATTN_PATTERNS[0][1] == 1 and all(d > 1 for _, d in ATTN_PATTERNS[1:])
    for g in reversed(range(N_GROUPS_ATTN)):
        dilation = ATTN_PATTERNS[g][1]
        q_ref, k_ref, v_ref = qkv[g]
        nb = seq // dilation // blk

        def per_block(idx, carry, g=g, dilation=dilation, q_ref=q_ref, k_ref=k_ref,
                      v_ref=v_ref, nb=nb):
            r, n = idx // nb, idx % nb
            row0 = pl.multiple_of(idx * blk, blk)
            prev0 = pl.multiple_of(jnp.maximum(row0 - blk, 0), blk)
            keys = jnp.concatenate([k_ref[pl.ds(prev0, blk), :],
                                    k_ref[pl.ds(row0, blk), :]], axis=0)
            vals = jnp.concatenate([v_ref[pl.ds(prev0, blk), :],
                                    v_ref[pl.ds(row0, blk), :]], axis=0)
            s = scores(q_ref[pl.ds(row0, blk), :], keys) + bias_ref[g, jnp.minimum(n, 1)]
            finish(g, dilation, r + dilation * n * blk, s, vals)
            return carry

        lax.fori_loop(0, dilation * nb, per_block, 0, unroll=ATTN_UNROLL)


def dilated_attention(planes, bias, batch, seq):
    hpg = HEADS_PER_GROUP

    def plane_spec(offset):
        return pl.BlockSpec((None, seq, LANES), lambda bi, h: (offset + h, bi, 0))

    in_specs, args = [], []
    for g in range(N_GROUPS_ATTN):
        for part in range(3):
            in_specs.append(plane_spec(part * hpg))
            args.append(planes[g])
    in_specs.append(plane_spec(3 * hpg))
    args.append(planes[0])
    in_specs.append(pl.BlockSpec((N_GROUPS_ATTN, None, 2, ATTN_BLOCK, 2 * ATTN_BLOCK),
                                 lambda bi, h: (0, h, 0, 0, 0)))
    args.append(bias)
    return pl.pallas_call(
        _attn_kernel,
        grid=(batch, hpg),
        in_specs=in_specs,
        out_specs=pl.BlockSpec((seq, LANES), lambda bi, h: (bi, h)),
        out_shape=jax.ShapeDtypeStruct((batch * seq, D_ATTN), BF16),
        scratch_shapes=[pltpu.VMEM((N_GROUPS_ATTN - 1, seq, LANES), F32),
                        pltpu.VMEM((N_GROUPS_ATTN - 1, seq, LANES), F32)],
        compiler_params=_cparams("parallel", "parallel"),
        name="dilated_attention",
    )(*args)


def _outproj_ln_kernel(y_ref, w_ref, x_ref, g_ref, b_ref, *refs, alpha, nt, nk):
    o_refs, accs = refs[:-2], refs[-2:]
    i, k = pl.program_id(0), pl.program_id(1)
    rc = x_ref.shape[0]
    d_model = w_ref.shape[1]
    n_ln = rc // LN_ROWS
    n_mm = n_ln if d_model % (n_ln * 2 * LANES) == 0 else 1
    cols = d_model // n_mm

    def matmul_part(acc_ref, first, c):
        sl = slice(c * cols, (c + 1) * cols)
        d = jnp.dot(y_ref[...], w_ref[:, sl], preferred_element_type=F32)
        if first:
            acc_ref[:, sl] = d
        else:
            acc_ref[:, sl] += d

    def layer_norm_part(done_ref, j):
        sub = pl.ds(j * LN_ROWS, LN_ROWS)
        src = pl.ds(pl.multiple_of(k * rc + j * LN_ROWS, LN_ROWS), LN_ROWS)
        v = done_ref[src, :] + alpha * x_ref[sub, :]
        mu = jnp.mean(v, axis=-1, keepdims=True)
        c = v - mu
        var = jnp.mean(c * c, axis=-1, keepdims=True)
        out = c * lax.rsqrt(var + LN_EPS) * g_ref[...] + b_ref[...]
        o_refs[0][sub, :] = out
        if len(o_refs) > 1:
            o_refs[1][sub, :] = out.astype(BF16)

    for parity in (0, 1):
        acc_ref, done_ref = accs[parity], accs[1 - parity]
        mine = (i % 2 == parity)
        for first in (True, False):
            cond = mine & ((k == 0) if first else (k > 0))

            if parity == 0:
                @pl.when(cond & (i == 0))
                def _(acc_ref=acc_ref, first=first):
                    for c in range(n_mm):
                        matmul_part(acc_ref, first, c)

            @pl.when(cond & (i > 0) & (i < nt))
            def _(acc_ref=acc_ref, done_ref=done_ref, first=first):
                for step in range(max(n_ln, n_mm)):
                    if step < n_mm:
                        matmul_part(acc_ref, first, step)
                    if step < n_ln:
                        layer_norm_part(done_ref, step)

        if parity == nt % 2:
            @pl.when(i == nt)
            def _(done_ref=done_ref):
                for j in range(n_ln):
                    layer_norm_part(done_ref, j)


def outproj_layernorm(y, w, x_res, gamma, beta, alpha, bm, bk, emit_bf16):
    m, kdim = y.shape
    d_model = w.shape[1]
    nt, nk = m // bm, kdim // bk
    assert nk >= 2
    rc = bm // nk

    def chunk(i, k):
        return (jnp.where(i == 0, 0, (i - 1) * nk + k), 0)

    out_shape = [jax.ShapeDtypeStruct((m, d_model), F32)]
    out_specs = [pl.BlockSpec((rc, d_model), chunk)]
    if emit_bf16:
        out_shape.append(jax.ShapeDtypeStruct((m, d_model), BF16))
        out_specs.append(pl.BlockSpec((rc, d_model), chunk))
    return pl.pallas_call(
        functools.partial(_outproj_ln_kernel, alpha=alpha, nt=nt, nk=nk),
        grid=(nt + 1, nk),
        in_specs=[
            pl.BlockSpec((bm, bk), lambda i, k: (jnp.minimum(i, nt - 1), k)),
            pl.BlockSpec((bk, d_model), lambda i, k: (jnp.where(i == nt, 0, k), 0)),
            pl.BlockSpec((rc, d_model), chunk),
            pl.BlockSpec((1, d_model), lambda i, k: (0, 0)),
            pl.BlockSpec((1, d_model), lambda i, k: (0, 0)),
        ],
        out_specs=out_specs,
        out_shape=out_shape,
        scratch_shapes=[pltpu.VMEM((bm, d_model), F32), pltpu.VMEM((bm, d_model), F32)],
        compiler_params=_cparams("arbitrary", "arbitrary"),
        name="outproj_layernorm",
    )(y, w, x_res, gamma.reshape(1, d_model).astype(F32), beta.reshape(1, d_model).astype(F32))


def _dt_kernel(x_ref, w_ref, bias_ref, a_ref, dt_ref, da_ref):
    w = w_ref[...]
    w_hi = w.astype(BF16)
    w_lo = (w - w_hi.astype(F32)).astype(BF16)
    x = x_ref[...]
    raw = (jnp.dot(x, w_hi, preferred_element_type=F32)
           + jnp.dot(x, w_lo, preferred_element_type=F32))
    v = raw + bias_ref[...]
    dt = jnp.maximum(v, 0.0) + jnp.log1p(jnp.exp(-jnp.abs(v)))
    dt_ref[...] = dt.T
    da_ref[...] = (dt * a_ref[...]).T


def dt_projection(x, w_in, dt_bias, a_log, bm):
    m, d_model = x.shape
    nh = dt_bias.shape[0]
    dt_block = w_in.shape[1] // nh - 1
    a_row = (-jnp.exp(a_log.astype(F32))).reshape(1, nh)
    return pl.pallas_call(
        _dt_kernel,
        grid=(m // bm,),
        in_specs=[
            pl.BlockSpec((bm, d_model), lambda i: (i, 0)),
            pl.BlockSpec((d_model, nh), lambda i: (0, dt_block)),
            pl.BlockSpec((1, nh), lambda i: (0, 0)),
            pl.BlockSpec((1, nh), lambda i: (0, 0)),
        ],
        out_specs=[pl.BlockSpec((nh, bm), lambda i: (0, i)),
                   pl.BlockSpec((nh, bm), lambda i: (0, i))],
        out_shape=[jax.ShapeDtypeStruct((nh, m), F32), jax.ShapeDtypeStruct((nh, m), F32)],
        compiler_params=_cparams("parallel"),
        name="dt_projection",
    )(x, w_in, dt_bias.astype(F32).reshape(1, nh), a_row)


DT_FLOOR = 1e-37


def _split3(v):
    hi = v.astype(BF16)
    rest = v - hi.astype(F32)
    mid = rest.astype(BF16)
    lo = (rest - mid.astype(F32)).astype(BF16)
    return hi, mid, lo


def _ssd_kernel(xs_ref, b_ref, c_ref, z_ref, dt_ref, da_ref,
                wx_ref, wb_ref, wc_ref, cbx_ref, cbb_ref, cbc_ref,
                dskip_ref, normw_ref, expand_ref, o_ref,
                state_sc, carry_ref, *, nsub):
    L = CHUNK
    nh = HEADS_PER_SSM_GROUP
    hp = lax.Precision.HIGHEST
    nt_dims = (((1,), (1,)), ((), ()))

    @pl.when(pl.program_id(2) == 0)
    def _():
        state_sc[...] = jnp.zeros_like(state_sc)
        carry_ref[...] = jnp.zeros_like(carry_ref)

    row_i = lax.broadcasted_iota(jnp.int32, (L, L), 0)
    col_i = lax.broadcasted_iota(jnp.int32, (L, L), 1)
    causal = row_i >= col_i
    triu = (row_i <= col_i).astype(F32)
    eye = (row_i == col_i).astype(F32)
    eye_b = eye.astype(BF16)
    lane = lax.broadcasted_iota(jnp.int32, (L, LANES), 1)
    head_lo = (lane < SSM_HEAD_DIM).astype(F32).astype(BF16)
    head_hi = (lane >= SSM_HEAD_DIM).astype(F32).astype(BF16)

    shift_mat = jnp.concatenate(
        [(col_i == row_i - s).astype(F32).astype(BF16) for s in range(1, CONV_WIDTH)],
        axis=0)
    conv_w = jnp.concatenate([wx_ref[...], wb_ref[...], wc_ref[...]], axis=1)
    conv_bias = jnp.concatenate([cbx_ref[...], cbb_ref[...], cbc_ref[...]], axis=1)
    tail_rows = carry_ref.shape[0]
    tail_row = lax.broadcasted_iota(jnp.int32, carry_ref.shape, 0)

    def conv_silu(cur):
        cur_f = cur.astype(F32)
        tail = carry_ref[...]
        carry_ref[...] = cur_f[L - tail_rows:, :]
        shifted = jnp.dot(shift_mat, cur, preferred_element_type=F32)
        acc = conv_bias + conv_w[CONV_WIDTH - 1:CONV_WIDTH, :] * cur_f
        head = jnp.zeros_like(tail)
        for s in range(1, CONV_WIDTH):
            k = CONV_WIDTH - 1 - s
            acc = acc + conv_w[k:k + 1, :] * shifted[(s - 1) * L:s * L, :]
            head = head + conv_w[k:k + 1, :] * jnp.where(tail_row < s,
                                                          pltpu.roll(tail, s, 0), 0.0)
        acc = jnp.concatenate([acc[:tail_rows] + head, acc[tail_rows:]], axis=0)
        return _silu(acc)

    def per_chunk(j, carry):
        rows = pl.ds(pl.multiple_of(j * L, L), L)
        xbc = conv_silu(jnp.concatenate([xs_ref[rows, :], b_ref[rows, :], c_ref[rows, :]],
                                        axis=1))
        xs = xbc[:, :GROUP_CH]
        bm = xbc[:, GROUP_CH:GROUP_CH + D_STATE]
        cm = xbc[:, GROUP_CH + D_STATE:]
        xs_b, bm_b, cm_b = xs.astype(BF16), bm.astype(BF16), cm.astype(BF16)

        dt_t = dt_ref[:, rows]
        a_t = jnp.dot(da_ref[:, rows], triu, precision=hp,
                      preferred_element_type=F32) * LOG2E
        log2dt = jnp.log(jnp.maximum(dt_t, DT_FLOOR)) * LOG2E
        a_last = a_t[:, L - 1:L]
        r_t = a_t - log2dt
        e_t = jnp.exp2(a_t)
        w_t = jnp.exp2(a_last - r_t)
        a_cols = lax.dot_general(eye, a_t, nt_dims, precision=hp,
                                 preferred_element_type=F32)
        parts = _split3(jnp.concatenate([e_t, w_t], axis=0))
        parts = jnp.concatenate(parts + (jnp.zeros_like(parts[0]),), axis=0)
        cols3 = lax.dot_general(eye_b, parts, nt_dims,
                                preferred_element_type=F32).astype(BF16)
        ew = jnp.dot(cols3, expand_ref[...], preferred_element_type=F32)
        e_exp, w_exp = ew[:, :GROUP_CH], ew[:, GROUP_CH:]

        cb = lax.dot_general(cm_b, bm_b, nt_dims, preferred_element_type=F32)
        state = state_sc[...]
        z_off = jnp.dot(cm_b, state.astype(BF16), preferred_element_type=F32)

        ys = []
        for c in range(PAIRS):
            lanes = slice(c * LANES, (c + 1) * LANES)
            x_c = xs_b[:, lanes]
            rhs = jnp.concatenate([x_c * head_lo, x_c * head_hi], axis=0)
            lhs = []
            for h in (2 * c, 2 * c + 1):
                seg = a_cols[:, h:h + 1] - r_t[h:h + 1, :]
                lhs.append((jnp.exp2(jnp.where(causal, seg, -jnp.inf)) * cb).astype(BF16))
            y_c = jnp.dot(jnp.concatenate(lhs, axis=1), rhs, preferred_element_type=F32)
            ys.append(y_c + e_exp[:, lanes] * z_off[:, lanes])
        y = jnp.concatenate(ys, axis=1)

        upd = lax.dot_general(bm_b, (xs * w_exp).astype(BF16),
                              (((0,), (0,)), ((), ())), preferred_element_type=F32)
        state_sc[...] = state * e_exp[L - 1:L, :] + upd

        y = y + dskip_ref[...] * xs
        y = y * _silu(z_ref[rows, :].astype(F32))
        ms = jnp.mean(y * y, axis=-1, keepdims=True)
        y = y * lax.rsqrt(ms + RMS_EPS) * normw_ref[...]
        o_ref[rows, :] = y.astype(o_ref.dtype)
        return carry

    lax.fori_loop(0, nsub, per_chunk, 0, unroll=SSD_CHUNK_UNROLL)


def ssd_mixer_core(proj, dt_t, da_t, conv_w, conv_b, d_skip, norm_w, batch, seq, rows_per_step):
    m = proj.shape[0]
    d_inner = SSM_GROUPS * GROUP_CH
    t = rows_per_step
    nsteps = seq // t
    xcol0 = d_inner // GROUP_CH
    bcol0 = 2 * d_inner // D_STATE
    ccol0 = bcol0 + SSM_GROUPS
    wb0 = d_inner // D_STATE
    wc0 = wb0 + SSM_GROUPS
    conv_w = conv_w.astype(F32)
    conv_b = conv_b.astype(F32).reshape(1, -1)
    d_exp = jnp.repeat(d_skip.astype(F32), SSM_HEAD_DIM).reshape(1, d_inner)
    norm_w = norm_w.astype(F32).reshape(1, d_inner)
    head_exp = jnp.repeat(jnp.eye(HEADS_PER_SSM_GROUP, dtype=F32), SSM_HEAD_DIM, axis=1)
    zeros = jnp.zeros_like(head_exp)
    pair = jnp.block([[head_exp, zeros], [zeros, head_exp]])
    expand = jnp.concatenate([pair, pair, pair, jnp.zeros_like(pair)], axis=0).astype(BF16)

    def row(bi, s):
        return bi * nsteps + s

    in_specs = [
        pl.BlockSpec((t, GROUP_CH), lambda bi, g, s: (row(bi, s), xcol0 + g)),
        pl.BlockSpec((t, D_STATE), lambda bi, g, s: (row(bi, s), bcol0 + g)),
        pl.BlockSpec((t, D_STATE), lambda bi, g, s: (row(bi, s), ccol0 + g)),
        pl.BlockSpec((t, GROUP_CH), lambda bi, g, s: (row(bi, s), g)),
        pl.BlockSpec((HEADS_PER_SSM_GROUP, t), lambda bi, g, s: (g, row(bi, s))),
        pl.BlockSpec((HEADS_PER_SSM_GROUP, t), lambda bi, g, s: (g, row(bi, s))),
        pl.BlockSpec((CONV_WIDTH, GROUP_CH), lambda bi, g, s: (0, g)),
        pl.BlockSpec((CONV_WIDTH, D_STATE), lambda bi, g, s: (0, wb0 + g)),
        pl.BlockSpec((CONV_WIDTH, D_STATE), lambda bi, g, s: (0, wc0 + g)),
        pl.BlockSpec((1, GROUP_CH), lambda bi, g, s: (0, g)),
        pl.BlockSpec((1, D_STATE), lambda bi, g, s: (0, wb0 + g)),
        pl.BlockSpec((1, D_STATE), lambda bi, g, s: (0, wc0 + g)),
        pl.BlockSpec((1, GROUP_CH), lambda bi, g, s: (0, g)),
        pl.BlockSpec((1, GROUP_CH), lambda bi, g, s: (0, g)),
        pl.BlockSpec(expand.shape, lambda bi, g, s: (0, 0)),
    ]
    return pl.pallas_call(
        functools.partial(_ssd_kernel, nsub=t // CHUNK),
        grid=(batch, SSM_GROUPS, nsteps),
        in_specs=in_specs,
        out_specs=pl.BlockSpec((t, GROUP_CH), lambda bi, g, s: (row(bi, s), g)),
        out_shape=jax.ShapeDtypeStruct((m, d_inner), BF16),
        scratch_shapes=[pltpu.VMEM((D_STATE, GROUP_CH), F32),
                        pltpu.VMEM((CONV_TAIL_ROWS, GROUP_CH + 2 * D_STATE), F32)],
        compiler_params=_cparams("parallel", "parallel", "arbitrary"),
        name="ssd_mixer_core",
    )(proj, proj, proj, proj, dt_t, da_t, conv_w, conv_w, conv_w, conv_b, conv_b, conv_b,
      d_exp, norm_w, expand)


def _attention_layer(x, w_in, w_out, rel_bias, gamma, beta, alpha, emit_bf16):
    batch, seq, d_model = x.shape
    dilations = tuple(d for _, d in ATTN_PATTERNS)
    group_cols = 3 * D_ATTN
    gate_col0 = N_GROUPS_ATTN * group_cols
    xperm = cast_permute(x, dilations, bm=CAST_ROWS)
    planes = []
    bm, bn = PROJ_TILE
    for g in range(N_GROUPS_ATTN):
        first = g * group_cols // bn
        if g == 0:
            nqkv, gate_first = group_cols // bn, gate_col0 // bn
            n = group_cols + D_ATTN
            col_block = lambda j, nqkv=nqkv, gate_first=gate_first: jnp.where(
                j < nqkv, j, j - nqkv + gate_first)
        else:
            n = group_cols
            col_block = lambda j, first=first: first + j
        planes.append(proj_planes(xperm[g], w_in.astype(F32), col_block, n, D_ATTN,
                                  HEAD_DIM ** -0.5 * LOG2E, bm, bn))
    y = dilated_attention(planes, masked_bias(rel_bias), batch, seq)
    return outproj_layernorm(y, w_out.astype(BF16), x.reshape(batch * seq, d_model),
                             gamma, beta, alpha, *OUTPROJ_TILE_ATTN, emit_bf16=emit_bf16)


def _ssd_layer(x, xb, w_in, conv_w, conv_b, dt_bias, a_log, d_skip, norm_w, w_out,
               gamma, beta, alpha):
    batch, seq, d_model = x.shape
    m = batch * seq
    n_main = w_in.shape[1] - dt_bias.shape[0]
    xf = x.reshape(m, d_model)
    proj = proj_flat(xb.reshape(m, d_model), w_in.astype(F32), n_main, *PROJ_TILE)
    dt_t, da_t = dt_projection(xb.reshape(m, d_model), w_in, dt_bias, a_log, bm=DT_ROWS)
    y = ssd_mixer_core(proj, dt_t, da_t, conv_w, conv_b, d_skip, norm_w, batch, seq,
                       rows_per_step=min(SSD_ROWS, seq))
    return outproj_layernorm(y, w_out.astype(BF16), xf, gamma, beta, alpha,
                             *OUTPROJ_TILE_SSM, emit_bf16=False)


def kernel(x, w_in_attn, w_out_attn, rel_bias, w_in_ssm, conv_w, conv_b, dt_bias, a_log,
           d_skip, ssm_norm_w, w_out_ssm, ln_g, ln_b):
    batch, seq, d_model = x.shape
    depth = ln_g.shape[0]
    alpha = (2 * depth) ** 0.25
    xb = None
    for i in range(depth):
        j = i // 2
        if i % 2 == 0:
            outs = _attention_layer(x, w_in_attn[j], w_out_attn[j], rel_bias,
                                    ln_g[i], ln_b[i], alpha, emit_bf16=i + 1 < depth)
            if i + 1 < depth:
                xb = outs[1].reshape(batch, seq, d_model)
        else:
            outs = _ssd_layer(x, xb, w_in_ssm[j], conv_w[j], conv_b[j], dt_bias[j], a_log[j],
                              d_skip[j], ssm_norm_w[j], w_out_ssm[j], ln_g[i], ln_b[i], alpha)
        x = outs[0].reshape(batch, seq, d_model)
    return x
```

```python
import functools
import math

import jax
import jax.numpy as jnp
from jax import lax
from jax.experimental import pallas as pl
from jax.experimental.pallas import tpu as pltpu

F32 = jnp.float32
BF16 = jnp.bfloat16

LANES = 128
BF16_ROWS = 16
VMEM_LIMIT_BYTES = 56 * 1024 * 1024

ATTN_PATTERNS = ((128, 1), (512, 4), (2048, 16))
N_GROUPS_ATTN = 3
HEADS_PER_GROUP = 16
HEAD_DIM = 128
D_ATTN = HEADS_PER_GROUP * HEAD_DIM
ATTN_BLOCK = 128
NUM_BUCKETS = 32
MAX_DISTANCE = 2048
SSM_HEAD_DIM = 64
SSM_GROUPS = 8
HEADS_PER_SSM_GROUP = 16
D_STATE = 128
CONV_WIDTH = 4
CHUNK = 128
LN_EPS = 1e-5
RMS_EPS = 1e-5
NEG_INF = -1e30
LOG2E = 1.4426950408889634

CAST_ROWS = 256
PROJ_TILE = (2048, 512)
ATTN_UNROLL = 32
OUTPROJ_ROWS = 512
OUTPROJ_K_ATTN = 512
OUTPROJ_K_SSM = 1024
LN_ROWS = 32
DT_ROWS = 1024
SSD_ROWS = 2048
SSD_CHUNK_UNROLL = 2

GROUP_CH = HEADS_PER_SSM_GROUP * SSM_HEAD_DIM
PAIRS = GROUP_CH // LANES
CONV_TAIL_ROWS = 8


def _cparams(*sem):
    return pltpu.CompilerParams(dimension_semantics=sem, vmem_limit_bytes=VMEM_LIMIT_BYTES)


def _cast_permute_kernel(x_ref, *o_refs, dilations):
    xb = x_ref[...].astype(BF16)
    bm = xb.shape[0]
    dst = lax.broadcasted_iota(jnp.int32, (bm, bm), 0)
    src = lax.broadcasted_iota(jnp.int32, (bm, bm), 1)
    for o_ref, d in zip(o_refs, dilations):
        rows = bm // d
        if d == 1:
            o_ref[0] = xb
            continue
        perm = (src == (dst % rows) * d + dst // rows).astype(BF16)
        xp = jnp.dot(perm, xb, preferred_element_type=F32).astype(BF16)
        for r in range(d):
            o_ref[r] = xp[r * rows:(r + 1) * rows, :]


def cast_permute(x, dilations, bm):
    b, s, d_model = x.shape
    return pl.pallas_call(
        functools.partial(_cast_permute_kernel, dilations=dilations),
        grid=(b, s // bm),
        in_specs=[pl.BlockSpec((None, bm, d_model), lambda bi, i: (bi, i, 0))],
        out_specs=[pl.BlockSpec((None, d, bm // d, d_model), lambda bi, i: (bi, 0, i, 0))
                   for d in dilations],
        out_shape=[jax.ShapeDtypeStruct((b, d, s // d, d_model), BF16) for d in dilations],
        compiler_params=_cparams("parallel", "parallel"),
        name="cast_permute",
    )(x)


def _silu(v):
    h = 0.5 * v
    return h + h * jnp.tanh(h)


def _proj_kernel(x_ref, w_ref, o_ref, *, q_blocks, q_scale):
    x = x_ref[...].reshape(-1, x_ref.shape[-1])
    acc = jnp.dot(x, w_ref[...].astype(BF16), preferred_element_type=F32)
    if q_blocks:
        acc = acc * jnp.where(pl.program_id(1) < q_blocks, q_scale, 1.0)
    if len(o_ref.shape) == 3:
        for c in range(o_ref.shape[0]):
            o_ref[c] = acc[:, c * LANES:(c + 1) * LANES].astype(o_ref.dtype)
    else:
        o_ref[...] = acc.astype(o_ref.dtype)


def proj_planes(xp, w, col_block, n, q_cols, q_scale, bm, bn):
    b, dilation, sub_len, d_model = xp.shape
    bm = min(bm, dilation * sub_len)
    tiles = dilation * sub_len // bm
    if sub_len >= bm:
        per_class = sub_len // bm
        x_spec = pl.BlockSpec(
            (None, 1, bm, d_model),
            lambda t, j: (t // tiles, (t % tiles) // per_class, (t % tiles) % per_class, 0),
            pipeline_mode=pl.Buffered(1))
    else:
        x_spec = pl.BlockSpec((None, bm // sub_len, sub_len, d_model),
                              lambda t, j: (t // tiles, t % tiles, 0, 0),
                              pipeline_mode=pl.Buffered(1))
    return pl.pallas_call(
        functools.partial(_proj_kernel, q_blocks=q_cols // bn, q_scale=q_scale),
        grid=(b * tiles, n // bn),
        in_specs=[x_spec, pl.BlockSpec((d_model, bn), lambda t, j: (0, col_block(j)))],
        out_specs=pl.BlockSpec((bn // LANES, bm, LANES), lambda t, j: (j, t, 0)),
        out_shape=jax.ShapeDtypeStruct((n // LANES, b * dilation * sub_len, LANES), BF16),
        compiler_params=_cparams("parallel", "arbitrary"),
        name=f"proj_planes_d{dilation}",
    )(xp, w)


def proj_flat(xb, w, n_out, bm, bn):
    m, d_model = xb.shape
    bm = min(bm, m)
    return pl.pallas_call(
        functools.partial(_proj_kernel, q_blocks=0, q_scale=1.0),
        grid=(m // bm, n_out // bn),
        in_specs=[
            pl.BlockSpec((bm, d_model), lambda i, j: (i, 0), pipeline_mode=pl.Buffered(1)),
            pl.BlockSpec((d_model, bn), lambda i, j: (0, j)),
        ],
        out_specs=pl.BlockSpec((bm, bn), lambda i, j: (i, j)),
        out_shape=jax.ShapeDtypeStruct((m, n_out), BF16),
        compiler_params=_cparams("parallel", "arbitrary"),
        name="proj_flat",
    )(xb, w)


def _t5_causal_bucket(dist):
    max_exact = NUM_BUCKETS // 2
    d_f = jnp.maximum(dist, 1).astype(F32)
    large = max_exact + (jnp.log(d_f / max_exact) / math.log(MAX_DISTANCE / max_exact)
                         * (NUM_BUCKETS - max_exact)).astype(jnp.int32)
    large = jnp.minimum(large, NUM_BUCKETS - 1)
    return jnp.where(dist < max_exact, dist, large)


def _band_buckets():
    qi = jnp.arange(ATTN_BLOCK)[:, None]
    ki = jnp.arange(2 * ATTN_BLOCK)[None, :]
    delta = ATTN_BLOCK + qi - ki
    out = []
    for window, dilation in ATTN_PATTERNS:
        span = window // dilation
        band = (delta >= 0) & (delta <= span)
        bucket = _t5_causal_bucket(jnp.clip(delta, 0, None) * dilation)
        out.append(jnp.where(band, bucket, -1))
    return jnp.stack(out).astype(jnp.int32)


def _bias_kernel(table_ref, bucket_ref, o_ref):
    g = pl.program_id(0)
    bucket = bucket_ref[...]
    prev_half = lax.broadcasted_iota(jnp.int32, bucket.shape, 1) < ATTN_BLOCK

    def per_head(h, carry):
        acc = jnp.full(bucket.shape, NEG_INF, F32)
        for b in range(NUM_BUCKETS):
            acc = jnp.where(bucket == b, table_ref[b, g * HEADS_PER_GROUP + h] * LOG2E, acc)
        o_ref[h, 0] = jnp.where(prev_half, NEG_INF, acc)
        o_ref[h, 1] = acc
        return carry

    lax.fori_loop(0, HEADS_PER_GROUP, per_head, 0)


def masked_bias(rel_bias):
    buckets = _band_buckets()
    return pl.pallas_call(
        _bias_kernel,
        grid=(N_GROUPS_ATTN,),
        in_specs=[
            pl.BlockSpec(memory_space=pltpu.SMEM),
            pl.BlockSpec((None, ATTN_BLOCK, 2 * ATTN_BLOCK), lambda g: (g, 0, 0)),
        ],
        out_specs=pl.BlockSpec((None, HEADS_PER_GROUP, 2, ATTN_BLOCK, 2 * ATTN_BLOCK),
                               lambda g: (g, 0, 0, 0, 0)),
        out_shape=jax.ShapeDtypeStruct(
            (N_GROUPS_ATTN, HEADS_PER_GROUP, 2, ATTN_BLOCK, 2 * ATTN_BLOCK), F32),
        compiler_params=_cparams("parallel"),
        name="masked_bias",
    )(rel_bias.astype(F32), buckets)


def _attn_kernel(q0, k0, v0, q1, k1, v1, q2, k2, v2, gate_ref, bias_ref, y_ref, o_sc, lse_sc):
    seq = y_ref.shape[0]
    blk = ATTN_BLOCK
    qkv = ((q0, k0, v0), (q1, k1, v1), (q2, k2, v2))

    def finish(g, dilation, tok0, s, vals):
        m = jnp.max(s, axis=-1, keepdims=True)
        p = jnp.exp2(s - m)
        denom = jnp.sum(p, axis=-1, keepdims=True)
        acc = jnp.dot(p.astype(BF16), vals, preferred_element_type=F32)
        o = acc * (1.0 / denom)
        lse = jnp.broadcast_to(m + jnp.log2(denom), (blk, LANES))
        if dilation > 1:
            rows = pl.ds(tok0, blk, stride=dilation)
            o_sc[g - 1, rows, :] = o
            lse_sc[g - 1, rows, :] = lse
            return
        rows = pl.ds(tok0, blk)
        l1, l2 = lse_sc[0, rows, :], lse_sc[1, rows, :]
        top = jnp.maximum(jnp.maximum(lse, l1), l2)
        w0, w1, w2 = jnp.exp2(lse - top), jnp.exp2(l1 - top), jnp.exp2(l2 - top)
        o = (w0 * o + w1 * o_sc[0, rows, :] + w2 * o_sc[1, rows, :]) * (1.0 / (w0 + w1 + w2))
        gate = gate_ref[rows, :].astype(F32)
        y_ref[rows, :] = (o * _silu(gate)).astype(y_ref.dtype)

    def scores(q, keys):
        return lax.dot_general(q, keys, (((1,), (1,)), ((), ())), preferred_element_type=F32)

    assert ATTN_PATTERNS[0][1] == 1 and all(d > 1 for _, d in ATTN_PATTERNS[1:])
    for g in reversed(range(N_GROUPS_ATTN)):
        dilation = ATTN_PATTERNS[g][1]
        q_ref, k_ref, v_ref = qkv[g]
        nb = seq // dilation // blk

        def per_block(idx, carry, g=g, dilation=dilation, q_ref=q_ref, k_ref=k_ref,
                      v_ref=v_ref, nb=nb):
            r, n = idx // nb, idx % nb
            row0 = pl.multiple_of(idx * blk, blk)
            prev0 = pl.multiple_of(jnp.maximum(row0 - blk, 0), blk)
            keys = jnp.concatenate([k_ref[pl.ds(prev0, blk), :],
                                    k_ref[pl.ds(row0, blk), :]], axis=0)
            vals = jnp.concatenate([v_ref[pl.ds(prev0, blk), :],
                                    v_ref[pl.ds(row0, blk), :]], axis=0)
            s = scores(q_ref[pl.ds(row0, blk), :], keys) + bias_ref[g, jnp.minimum(n, 1)]
            finish(g, dilation, r + dilation * n * blk, s, vals)
            return carry

        lax.fori_loop(0, dilation * nb, per_block, 0, unroll=ATTN_UNROLL)


def dilated_attention(planes, bias, batch, seq):
    hpg = HEADS_PER_GROUP

    def plane_spec(offset):
        return pl.BlockSpec((None, seq, LANES), lambda bi, h: (offset + h, bi, 0))

    in_specs, args = [], []
    for g in range(N_GROUPS_ATTN):
        for part in range(3):
            in_specs.append(plane_spec(part * hpg))
            args.append(planes[g])
    in_specs.append(plane_spec(3 * hpg))
    args.append(planes[0])
    in_specs.append(pl.BlockSpec((N_GROUPS_ATTN, None, 2, ATTN_BLOCK, 2 * ATTN_BLOCK),
                                 lambda bi, h: (0, h, 0, 0, 0)))
    args.append(bias)
    return pl.pallas_call(
        _attn_kernel,
        grid=(batch, hpg),
        in_specs=in_specs,
        out_specs=pl.BlockSpec((seq, LANES), lambda bi, h: (bi, h)),
        out_shape=jax.ShapeDtypeStruct((batch * seq, D_ATTN), BF16),
        scratch_shapes=[pltpu.VMEM((N_GROUPS_ATTN - 1, seq, LANES), F32),
                        pltpu.VMEM((N_GROUPS_ATTN - 1, seq, LANES), F32)],
        compiler_params=_cparams("parallel", "parallel"),
        name="dilated_attention",
    )(*args)


def _outproj_ln_kernel(y_ref, w_ref, x_ref, g_ref, b_ref, *refs, alpha, nt, nk):
    o_refs, accs = refs[:-2], refs[-2:]
    i, k = pl.program_id(0), pl.program_id(1)
    rc = x_ref.shape[0]
    d_model = w_ref.shape[1]
    n_ln = rc // LN_ROWS
    n_mm = n_ln if d_model % (n_ln * 2 * LANES) == 0 else 1
    cols = d_model // n_mm

    def matmul_part(acc_ref, first, c, anchor=None):
        sl = slice(c * cols, (c + 1) * cols)
        y = y_ref[...]
        if anchor is not None:
            tile = (BF16_ROWS, LANES)
            zero = (anchor[:tile[0], :tile[1]] * 0.0).astype(BF16)
            top = jnp.concatenate([y[:tile[0], :tile[1]] + zero, y[:tile[0], tile[1]:]], axis=1)
            y = jnp.concatenate([top, y[tile[0]:, :]], axis=0)
        d = jnp.dot(y, w_ref[:, sl], preferred_element_type=F32)
        if first:
            acc_ref[:, sl] = d
        else:
            acc_ref[:, sl] += d

    def layer_norm_part(done_ref, j):
        sub = pl.ds(j * LN_ROWS, LN_ROWS)
        src = pl.ds(pl.multiple_of(k * rc + j * LN_ROWS, LN_ROWS), LN_ROWS)
        v = done_ref[src, :] + alpha * x_ref[sub, :]
        mu = jnp.mean(v, axis=-1, keepdims=True)
        c = v - mu
        var = jnp.mean(c * c, axis=-1, keepdims=True)
        out = c * lax.rsqrt(var + LN_EPS) * g_ref[...] + b_ref[...]
        o_refs[0][sub, :] = out
        if len(o_refs) > 1:
            o_refs[1][sub, :] = out.astype(BF16)
        return out

    for parity in (0, 1):
        acc_ref, done_ref = accs[parity], accs[1 - parity]
        mine = (i % 2 == parity)
        for first in (True, False):
            cond = mine & ((k == 0) if first else (k > 0))

            if parity == 0:
                @pl.when(cond & (i == 0))
                def _(acc_ref=acc_ref, first=first):
                    for c in range(n_mm):
                        matmul_part(acc_ref, first, c)

            @pl.when(cond & (i > 0) & (i < nt))
            def _(acc_ref=acc_ref, done_ref=done_ref, first=first):
                matmul_part(acc_ref, first, 0)
                anchor = layer_norm_part(done_ref, 0)
                for step in range(1, max(n_ln, n_mm)):
                    if step < n_ln:
                        anchor = layer_norm_part(done_ref, step)
                    if step < n_mm:
                        matmul_part(acc_ref, first, step, anchor)

        if parity == nt % 2:
            @pl.when(i == nt)
            def _(done_ref=done_ref):
                for j in range(n_ln):
                    layer_norm_part(done_ref, j)


def outproj_layernorm(y, w, x_res, gamma, beta, alpha, bm, bk, emit_bf16):
    m, kdim = y.shape
    d_model = w.shape[1]
    nt, nk = m // bm, kdim // bk
    assert nk >= 2
    rc = bm // nk

    def chunk(i, k):
        return (jnp.where(i == 0, 0, (i - 1) * nk + k), 0)

    out_shape = [jax.ShapeDtypeStruct((m, d_model), F32)]
    out_specs = [pl.BlockSpec((rc, d_model), chunk)]
    if emit_bf16:
        out_shape.append(jax.ShapeDtypeStruct((m, d_model), BF16))
        out_specs.append(pl.BlockSpec((rc, d_model), chunk))
    return pl.pallas_call(
        functools.partial(_outproj_ln_kernel, alpha=alpha, nt=nt, nk=nk),
        grid=(nt + 1, nk),
        in_specs=[
            pl.BlockSpec((bm, bk), lambda i, k: (jnp.minimum(i, nt - 1), k)),
            pl.BlockSpec((bk, d_model), lambda i, k: (jnp.where(i == nt, 0, k), 0)),
            pl.BlockSpec((rc, d_model), chunk),
            pl.BlockSpec((1, d_model), lambda i, k: (0, 0)),
            pl.BlockSpec((1, d_model), lambda i, k: (0, 0)),
        ],
        out_specs=out_specs,
        out_shape=out_shape,
        scratch_shapes=[pltpu.VMEM((bm, d_model), F32), pltpu.VMEM((bm, d_model), F32)],
        compiler_params=_cparams("arbitrary", "arbitrary"),
        name="outproj_layernorm",
    )(y, w, x_res, gamma.reshape(1, d_model).astype(F32), beta.reshape(1, d_model).astype(F32))


def _dt_kernel(x_ref, w_ref, bias_ref, a_ref, dt_ref, da_ref):
    w = w_ref[...]
    w_hi = w.astype(BF16)
    w_lo = (w - w_hi.astype(F32)).astype(BF16)
    x = x_ref[...]
    raw = (jnp.dot(x, w_hi, preferred_element_type=F32)
           + jnp.dot(x, w_lo, preferred_element_type=F32))
    v = raw + bias_ref[...]
    dt = jnp.maximum(v, 0.0) + jnp.log1p(jnp.exp(-jnp.abs(v)))
    dt_ref[...] = dt.T
    da_ref[...] = (dt * a_ref[...]).T


def dt_projection(x, w_in, dt_bias, a_log, bm):
    m, d_model = x.shape
    nh = dt_bias.shape[0]
    dt_block = w_in.shape[1] // nh - 1
    a_row = (-jnp.exp(a_log.astype(F32))).reshape(1, nh)
    return pl.pallas_call(
        _dt_kernel,
        grid=(m // bm,),
        in_specs=[
            pl.BlockSpec((bm, d_model), lambda i: (i, 0)),
            pl.BlockSpec((d_model, nh), lambda i: (0, dt_block)),
            pl.BlockSpec((1, nh), lambda i: (0, 0)),
            pl.BlockSpec((1, nh), lambda i: (0, 0)),
        ],
        out_specs=[pl.BlockSpec((nh, bm), lambda i: (0, i)),
                   pl.BlockSpec((nh, bm), lambda i: (0, i))],
        out_shape=[jax.ShapeDtypeStruct((nh, m), F32), jax.ShapeDtypeStruct((nh, m), F32)],
        compiler_params=_cparams("parallel"),
        name="dt_projection",
    )(x, w_in, dt_bias.astype(F32).reshape(1, nh), a_row)


DT_FLOOR = 1e-37


def _split3(v):
    hi = v.astype(BF16)
    rest = v - hi.astype(F32)
    mid = rest.astype(BF16)
    lo = (rest - mid.astype(F32)).astype(BF16)
    return hi, mid, lo


def _ssd_kernel(xs_ref, b_ref, c_ref, z_ref, dt_ref, da_ref,
                wx_ref, wb_ref, wc_ref, cbx_ref, cbb_ref, cbc_ref,
                dskip_ref, normw_ref, expand_ref, o_ref,
                state_sc, carry_ref, *, nsub):
    L = CHUNK
    nh = HEADS_PER_SSM_GROUP
    hp = lax.Precision.HIGHEST
    nt_dims = (((1,), (1,)), ((), ()))

    @pl.when(pl.program_id(2) == 0)
    def _():
        state_sc[...] = jnp.zeros_like(state_sc)
        carry_ref[...] = jnp.zeros_like(carry_ref)

    row_i = lax.broadcasted_iota(jnp.int32, (L, L), 0)
    col_i = lax.broadcasted_iota(jnp.int32, (L, L), 1)
    causal = row_i >= col_i
    triu = (row_i <= col_i).astype(F32)
    eye = (row_i == col_i).astype(F32)
    eye_b = eye.astype(BF16)
    lane = lax.broadcasted_iota(jnp.int32, (L, LANES), 1)
    head_lo = (lane < SSM_HEAD_DIM).astype(F32).astype(BF16)
    head_hi = (lane >= SSM_HEAD_DIM).astype(F32).astype(BF16)

    shift_mat = jnp.concatenate(
        [(col_i == row_i - s).astype(F32).astype(BF16) for s in range(1, CONV_WIDTH)],
        axis=0)
    conv_w = jnp.concatenate([wx_ref[...], wb_ref[...], wc_ref[...]], axis=1)
    conv_bias = jnp.concatenate([cbx_ref[...], cbb_ref[...], cbc_ref[...]], axis=1)
    tail_rows = carry_ref.shape[0]
    tail_row = lax.broadcasted_iota(jnp.int32, carry_ref.shape, 0)

    def conv_silu(cur):
        cur_f = cur.astype(F32)
        tail = carry_ref[...]
        carry_ref[...] = cur_f[L - tail_rows:, :]
        shifted = jnp.dot(shift_mat, cur, preferred_element_type=F32)
        acc = conv_bias + conv_w[CONV_WIDTH - 1:CONV_WIDTH, :] * cur_f
        head = jnp.zeros_like(tail)
        for s in range(1, CONV_WIDTH):
            k = CONV_WIDTH - 1 - s
            acc = acc + conv_w[k:k + 1, :] * shifted[(s - 1) * L:s * L, :]
            head = head + conv_w[k:k + 1, :] * jnp.where(tail_row < s,
                                                          pltpu.roll(tail, s, 0), 0.0)
        acc = jnp.concatenate([acc[:tail_rows] + head, acc[tail_rows:]], axis=0)
        return _silu(acc)

    def per_chunk(j, carry):
        rows = pl.ds(pl.multiple_of(j * L, L), L)
        xbc = conv_silu(jnp.concatenate([xs_ref[rows, :], b_ref[rows, :], c_ref[rows, :]],
                                        axis=1))
        xs = xbc[:, :GROUP_CH]
        bm = xbc[:, GROUP_CH:GROUP_CH + D_STATE]
        cm = xbc[:, GROUP_CH + D_STATE:]
        xs_b, bm_b, cm_b = xs.astype(BF16), bm.astype(BF16), cm.astype(BF16)

        dt_t = dt_ref[:, rows]
        a_t = jnp.dot(da_ref[:, rows], triu, precision=hp,
                      preferred_element_type=F32) * LOG2E
        log2dt = jnp.log(jnp.maximum(dt_t, DT_FLOOR)) * LOG2E
        a_last = a_t[:, L - 1:L]
        r_t = a_t - log2dt
        e_t = jnp.exp2(a_t)
        w_t = jnp.exp2(a_last - r_t)
        a_cols = lax.dot_general(eye, a_t, nt_dims, precision=hp,
                                 preferred_element_type=F32)
        parts = _split3(jnp.concatenate([e_t, w_t], axis=0))
        parts = jnp.concatenate(parts + (jnp.zeros_like(parts[0]),), axis=0)
        cols3 = lax.dot_general(eye_b, parts, nt_dims,
                                preferred_element_type=F32).astype(BF16)
        ew = jnp.dot(cols3, expand_ref[...], preferred_element_type=F32)
        e_exp, w_exp = ew[:, :GROUP_CH], ew[:, GROUP_CH:]

        cb = lax.dot_general(cm_b, bm_b, nt_dims, preferred_element_type=F32)
        state = state_sc[...]
        z_off = jnp.dot(cm_b, state.astype(BF16), preferred_element_type=F32)

        ys = []
        for c in range(PAIRS):
            lanes = slice(c * LANES, (c + 1) * LANES)
            x_c = xs_b[:, lanes]
            rhs = jnp.concatenate([x_c * head_lo, x_c * head_hi], axis=0)
            lhs = []
            for h in (2 * c, 2 * c + 1):
                seg = a_cols[:, h:h + 1] - r_t[h:h + 1, :]
                lhs.append((jnp.exp2(jnp.where(causal, seg, -jnp.inf)) * cb).astype(BF16))
            y_c = jnp.dot(jnp.concatenate(lhs, axis=1), rhs, preferred_element_type=F32)
            ys.append(y_c + e_exp[:, lanes] * z_off[:, lanes])
        y = jnp.concatenate(ys, axis=1)

        upd = lax.dot_general(bm_b, (xs * w_exp).astype(BF16),
                              (((0,), (0,)), ((), ())), preferred_element_type=F32)
        state_sc[...] = state * e_exp[L - 1:L, :] + upd

        y = y + dskip_ref[...] * xs
        y = y * _silu(z_ref[rows, :].astype(F32))
        ms = jnp.mean(y * y, axis=-1, keepdims=True)
        y = y * lax.rsqrt(ms + RMS_EPS) * normw_ref[...]
        o_ref[rows, :] = y.astype(o_ref.dtype)
        return carry

    lax.fori_loop(0, nsub, per_chunk, 0, unroll=SSD_CHUNK_UNROLL)


def ssd_mixer_core(proj, dt_t, da_t, conv_w, conv_b, d_skip, norm_w, batch, seq, rows_per_step):
    m = proj.shape[0]
    d_inner = SSM_GROUPS * GROUP_CH
    t = rows_per_step
    nsteps = seq // t
    xcol0 = d_inner // GROUP_CH
    bcol0 = 2 * d_inner // D_STATE
    ccol0 = bcol0 + SSM_GROUPS
    wb0 = d_inner // D_STATE
    wc0 = wb0 + SSM_GROUPS
    conv_w = conv_w.astype(F32)
    conv_b = conv_b.astype(F32).reshape(1, -1)
    d_exp = jnp.repeat(d_skip.astype(F32), SSM_HEAD_DIM).reshape(1, d_inner)
    norm_w = norm_w.astype(F32).reshape(1, d_inner)
    head_exp = jnp.repeat(jnp.eye(HEADS_PER_SSM_GROUP, dtype=F32), SSM_HEAD_DIM, axis=1)
    zeros = jnp.zeros_like(head_exp)
    pair = jnp.block([[head_exp, zeros], [zeros, head_exp]])
    expand = jnp.concatenate([pair, pair, pair, jnp.zeros_like(pair)], axis=0).astype(BF16)

    def row(bi, s):
        return bi * nsteps + s

    in_specs = [
        pl.BlockSpec((t, GROUP_CH), lambda bi, g, s: (row(bi, s), xcol0 + g)),
        pl.BlockSpec((t, D_STATE), lambda bi, g, s: (row(bi, s), bcol0 + g)),
        pl.BlockSpec((t, D_STATE), lambda bi, g, s: (row(bi, s), ccol0 + g)),
        pl.BlockSpec((t, GROUP_CH), lambda bi, g, s: (row(bi, s), g)),
        pl.BlockSpec((HEADS_PER_SSM_GROUP, t), lambda bi, g, s: (g, row(bi, s))),
        pl.BlockSpec((HEADS_PER_SSM_GROUP, t), lambda bi, g, s: (g, row(bi, s))),
        pl.BlockSpec((CONV_WIDTH, GROUP_CH), lambda bi, g, s: (0, g)),
        pl.BlockSpec((CONV_WIDTH, D_STATE), lambda bi, g, s: (0, wb0 + g)),
        pl.BlockSpec((CONV_WIDTH, D_STATE), lambda bi, g, s: (0, wc0 + g)),
        pl.BlockSpec((1, GROUP_CH), lambda bi, g, s: (0, g)),
        pl.BlockSpec((1, D_STATE), lambda bi, g, s: (0, wb0 + g)),
        pl.BlockSpec((1, D_STATE), lambda bi, g, s: (0, wc0 + g)),
        pl.BlockSpec((1, GROUP_CH), lambda bi, g, s: (0, g)),
        pl.BlockSpec((1, GROUP_CH), lambda bi, g, s: (0, g)),
        pl.BlockSpec(expand.shape, lambda bi, g, s: (0, 0)),
    ]
    return pl.pallas_call(
        functools.partial(_ssd_kernel, nsub=t // CHUNK),
        grid=(batch, SSM_GROUPS, nsteps),
        in_specs=in_specs,
        out_specs=pl.BlockSpec((t, GROUP_CH), lambda bi, g, s: (row(bi, s), g)),
        out_shape=jax.ShapeDtypeStruct((m, d_inner), BF16),
        scratch_shapes=[pltpu.VMEM((D_STATE, GROUP_CH), F32),
                        pltpu.VMEM((CONV_TAIL_ROWS, GROUP_CH + 2 * D_STATE), F32)],
        compiler_params=_cparams("parallel", "parallel", "arbitrary"),
        name="ssd_mixer_core",
    )(proj, proj, proj, proj, dt_t, da_t, conv_w, conv_w, conv_w, conv_b, conv_b, conv_b,
      d_exp, norm_w, expand)


def _attention_layer(x, w_in, w_out, rel_bias, gamma, beta, alpha, emit_bf16):
    batch, seq, d_model = x.shape
    dilations = tuple(d for _, d in ATTN_PATTERNS)
    group_cols = 3 * D_ATTN
    gate_col0 = N_GROUPS_ATTN * group_cols
    xperm = cast_permute(x, dilations, bm=CAST_ROWS)
    planes = []
    bm, bn = PROJ_TILE
    for g in range(N_GROUPS_ATTN):
        first = g * group_cols // bn
        if g == 0:
            nqkv, gate_first = group_cols // bn, gate_col0 // bn
            n = group_cols + D_ATTN
            col_block = lambda j, nqkv=nqkv, gate_first=gate_first: jnp.where(
                j < nqkv, j, j - nqkv + gate_first)
        else:
            n = group_cols
            col_block = lambda j, first=first: first + j
        planes.append(proj_planes(xperm[g], w_in.astype(F32), col_block, n, D_ATTN,
                                  HEAD_DIM ** -0.5 * LOG2E, bm, bn))
    y = dilated_attention(planes, masked_bias(rel_bias), batch, seq)
    return outproj_layernorm(y, w_out.astype(BF16), x.reshape(batch * seq, d_model),
                             gamma, beta, alpha, bm=OUTPROJ_ROWS, bk=OUTPROJ_K_ATTN,
                             emit_bf16=emit_bf16)


def _ssd_layer(x, xb, w_in, conv_w, conv_b, dt_bias, a_log, d_skip, norm_w, w_out,
               gamma, beta, alpha):
    batch, seq, d_model = x.shape
    m = batch * seq
    n_main = w_in.shape[1] - dt_bias.shape[0]
    xf = x.reshape(m, d_model)
    proj = proj_flat(xb.reshape(m, d_model), w_in.astype(F32), n_main, *PROJ_TILE)
    dt_t, da_t = dt_projection(xb.reshape(m, d_model), w_in, dt_bias, a_log, bm=DT_ROWS)
    y = ssd_mixer_core(proj, dt_t, da_t, conv_w, conv_b, d_skip, norm_w, batch, seq,
                       rows_per_step=min(SSD_ROWS, seq))
    return outproj_layernorm(y, w_out.astype(BF16), xf, gamma, beta, alpha,
                             bm=OUTPROJ_ROWS, bk=OUTPROJ_K_SSM, emit_bf16=False)


def kernel(x, w_in_attn, w_out_attn, rel_bias, w_in_ssm, conv_w, conv_b, dt_bias, a_log,
           d_skip, ssm_norm_w, w_out_ssm, ln_g, ln_b):
    batch, seq, d_model = x.shape
    depth = ln_g.shape[0]
    alpha = (2 * depth) ** 0.25
    xb = None
    for i in range(depth):
        j = i // 2
        if i % 2 == 0:
            outs = _attention_layer(x, w_in_attn[j], w_out_attn[j], rel_bias,
                                    ln_g[i], ln_b[i], alpha, emit_bf16=i + 1 < depth)
            if i + 1 < depth:
                xb = outs[1].reshape(batch, seq, d_model)
        else:
            outs = _ssd_layer(x, xb, w_in_ssm[j], conv_w[j], conv_b[j], dt_bias[j], a_log[j],
                              d_skip[j], ssm_norm_w[j], w_out_ssm[j], ln_g[i], ln_b[i], alpha)
        x = outs[0].reshape(batch, seq, d_model)
    return x
```

```python
import functools
import math

import jax
import jax.numpy as jnp
from jax import lax
from jax.experimental import pallas as pl
from jax.experimental.pallas import tpu as pltpu

F32 = jnp.float32
BF16 = jnp.bfloat16

LANES = 128
BF16_ROWS = 16
VMEM_LIMIT_BYTES = 56 * 1024 * 1024

ATTN_PATTERNS = ((128, 1), (512, 4), (2048, 16))
N_GROUPS_ATTN = 3
HEADS_PER_GROUP = 16
HEAD_DIM = 128
D_ATTN = HEADS_PER_GROUP * HEAD_DIM
ATTN_BLOCK = 128
NUM_BUCKETS = 32
MAX_DISTANCE = 2048
SSM_HEAD_DIM = 64
SSM_GROUPS = 8
HEADS_PER_SSM_GROUP = 16
D_STATE = 128
CONV_WIDTH = 4
CHUNK = 128
LN_EPS = 1e-5
RMS_EPS = 1e-5
NEG_INF = -1e30
LOG2E = 1.4426950408889634

CAST_ROWS = 256
PROJ_TILE = (2048, 512)
ATTN_UNROLL = 32
OUTPROJ_TILE_ATTN = (256, 2048)
OUTPROJ_TILE_SSM = (512, 1024)
LN_ROWS = 32
DT_ROWS = 1024
SSD_ROWS = 2048
SSD_CHUNK_UNROLL = 2

GROUP_CH = HEADS_PER_SSM_GROUP * SSM_HEAD_DIM
PAIRS = GROUP_CH // LANES
CONV_TAIL_ROWS = 8


def _cparams(*sem):
    return pltpu.CompilerParams(dimension_semantics=sem, vmem_limit_bytes=VMEM_LIMIT_BYTES)


def _cast_permute_kernel(x_ref, *o_refs, dilations):
    xb = x_ref[...].astype(BF16)
    bm = xb.shape[0]
    dst = lax.broadcasted_iota(jnp.int32, (bm, bm), 0)
    src = lax.broadcasted_iota(jnp.int32, (bm, bm), 1)
    for o_ref, d in zip(o_refs, dilations):
        rows = bm // d
        if d == 1:
            o_ref[0] = xb
            continue
        perm = (src == (dst % rows) * d + dst // rows).astype(BF16)
        xp = jnp.dot(perm, xb, preferred_element_type=F32).astype(BF16)
        for r in range(d):
            o_ref[r] = xp[r * rows:(r + 1) * rows, :]


def cast_permute(x, dilations, bm):
    b, s, d_model = x.shape
    return pl.pallas_call(
        functools.partial(_cast_permute_kernel, dilations=dilations),
        grid=(b, s // bm),
        in_specs=[pl.BlockSpec((None, bm, d_model), lambda bi, i: (bi, i, 0))],
        out_specs=[pl.BlockSpec((None, d, bm // d, d_model), lambda bi, i: (bi, 0, i, 0))
                   for d in dilations],
        out_shape=[jax.ShapeDtypeStruct((b, d, s // d, d_model), BF16) for d in dilations],
        compiler_params=_cparams("parallel", "parallel"),
        name="cast_permute",
    )(x)


def _silu(v):
    h = 0.5 * v
    return h + h * jnp.tanh(h)


def _proj_kernel(x_ref, w_ref, o_ref, *, q_blocks, q_scale):
    x = x_ref[...].reshape(-1, x_ref.shape[-1])
    acc = jnp.dot(x, w_ref[...].astype(BF16), preferred_element_type=F32)
    if q_blocks:
        acc = acc * jnp.where(pl.program_id(1) < q_blocks, q_scale, 1.0)
    if len(o_ref.shape) == 3:
        for c in range(o_ref.shape[0]):
            o_ref[c] = acc[:, c * LANES:(c + 1) * LANES].astype(o_ref.dtype)
    else:
        o_ref[...] = acc.astype(o_ref.dtype)


def proj_planes(xp, w, col_block, n, q_cols, q_scale, bm, bn):
    b, dilation, sub_len, d_model = xp.shape
    bm = min(bm, dilation * sub_len)
    tiles = dilation * sub_len // bm
    if sub_len >= bm:
        per_class = sub_len // bm
        x_spec = pl.BlockSpec(
            (None, 1, bm, d_model),
            lambda t, j: (t // tiles, (t % tiles) // per_class, (t % tiles) % per_class, 0),
            pipeline_mode=pl.Buffered(1))
    else:
        x_spec = pl.BlockSpec((None, bm // sub_len, sub_len, d_model),
                              lambda t, j: (t // tiles, t % tiles, 0, 0),
                              pipeline_mode=pl.Buffered(1))
    return pl.pallas_call(
        functools.partial(_proj_kernel, q_blocks=q_cols // bn, q_scale=q_scale),
        grid=(b * tiles, n // bn),
        in_specs=[x_spec, pl.BlockSpec((d_model, bn), lambda t, j: (0, col_block(j)))],
        out_specs=pl.BlockSpec((bn // LANES, bm, LANES), lambda t, j: (j, t, 0)),
        out_shape=jax.ShapeDtypeStruct((n // LANES, b * dilation * sub_len, LANES), BF16),
        compiler_params=_cparams("parallel", "arbitrary"),
        name=f"proj_planes_d{dilation}",
    )(xp, w)


def proj_flat(xb, w, n_out, bm, bn):
    m, d_model = xb.shape
    bm = min(bm, m)
    return pl.pallas_call(
        functools.partial(_proj_kernel, q_blocks=0, q_scale=1.0),
        grid=(m // bm, n_out // bn),
        in_specs=[
            pl.BlockSpec((bm, d_model), lambda i, j: (i, 0), pipeline_mode=pl.Buffered(1)),
            pl.BlockSpec((d_model, bn), lambda i, j: (0, j)),
        ],
        out_specs=pl.BlockSpec((bm, bn), lambda i, j: (i, j)),
        out_shape=jax.ShapeDtypeStruct((m, n_out), BF16),
        compiler_params=_cparams("parallel", "arbitrary"),
        name="proj_flat",
    )(xb, w)


def _t5_causal_bucket(dist):
    max_exact = NUM_BUCKETS // 2
    d_f = jnp.maximum(dist, 1).astype(F32)
    large = max_exact + (jnp.log(d_f / max_exact) / math.log(MAX_DISTANCE / max_exact)
                         * (NUM_BUCKETS - max_exact)).astype(jnp.int32)
    large = jnp.minimum(large, NUM_BUCKETS - 1)
    return jnp.where(dist < max_exact, dist, large)


def _band_buckets():
    qi = jnp.arange(ATTN_BLOCK)[:, None]
    ki = jnp.arange(2 * ATTN_BLOCK)[None, :]
    delta = ATTN_BLOCK + qi - ki
    out = []
    for window, dilation in ATTN_PATTERNS:
        span = window // dilation
        band = (delta >= 0) & (delta <= span)
        bucket = _t5_causal_bucket(jnp.clip(delta, 0, None) * dilation)
        out.append(jnp.where(band, bucket, -1))
    return jnp.stack(out).astype(jnp.int32)


def _bias_kernel(table_ref, bucket_ref, o_ref):
    g = pl.program_id(0)
    bucket = bucket_ref[...]
    prev_half = lax.broadcasted_iota(jnp.int32, bucket.shape, 1) < ATTN_BLOCK

    def per_head(h, carry):
        acc = jnp.full(bucket.shape, NEG_INF, F32)
        for b in range(NUM_BUCKETS):
            acc = jnp.where(bucket == b, table_ref[b, g * HEADS_PER_GROUP + h] * LOG2E, acc)
        o_ref[h, 0] = jnp.where(prev_half, NEG_INF, acc)
        o_ref[h, 1] = acc
        return carry

    lax.fori_loop(0, HEADS_PER_GROUP, per_head, 0)


def masked_bias(rel_bias):
    buckets = _band_buckets()
    return pl.pallas_call(
        _bias_kernel,
        grid=(N_GROUPS_ATTN,),
        in_specs=[
            pl.BlockSpec(memory_space=pltpu.SMEM),
            pl.BlockSpec((None, ATTN_BLOCK, 2 * ATTN_BLOCK), lambda g: (g, 0, 0)),
        ],
        out_specs=pl.BlockSpec((None, HEADS_PER_GROUP, 2, ATTN_BLOCK, 2 * ATTN_BLOCK),
                               lambda g: (g, 0, 0, 0, 0)),
        out_shape=jax.ShapeDtypeStruct(
            (N_GROUPS_ATTN, HEADS_PER_GROUP, 2, ATTN_BLOCK, 2 * ATTN_BLOCK), F32),
        compiler_params=_cparams("parallel"),
        name="masked_bias",
    )(rel_bias.astype(F32), buckets)


def _attn_kernel(q0, k0, v0, q1, k1, v1, q2, k2, v2, gate_ref, bias_ref, y_ref, o_sc, lse_sc):
    seq = y_ref.shape[0]
    blk = ATTN_BLOCK
    qkv = ((q0, k0, v0), (q1, k1, v1), (q2, k2, v2))

    def finish(g, dilation, tok0, s, vals):
        m = jnp.max(s, axis=-1, keepdims=True)
        p = jnp.exp2(s - m)
        denom = jnp.sum(p, axis=-1, keepdims=True)
        acc = jnp.dot(p.astype(BF16), vals, preferred_element_type=F32)
        o = acc * (1.0 / denom)
        lse = jnp.broadcast_to(m + jnp.log2(denom), (blk, LANES))
        if dilation > 1:
            rows = pl.ds(tok0, blk, stride=dilation)
            o_sc[g - 1, rows, :] = o
            lse_sc[g - 1, rows, :] = lse
            return
        rows = pl.ds(tok0, blk)
        l1, l2 = lse_sc[0, rows, :], lse_sc[1, rows, :]
        top = jnp.maximum(jnp.maximum(lse, l1), l2)
        w0, w1, w2 = jnp.exp2(lse - top), jnp.exp2(l1 - top), jnp.exp2(l2 - top)
        o = (w0 * o + w1 * o_sc[0, rows, :] + w2 * o_sc[1, rows, :]) * (1.0 / (w0 + w1 + w2))
        gate = gate_ref[rows, :].astype(F32)
        y_ref[rows, :] = (o * _silu(gate)).astype(y_ref.dtype)

    def scores(q, keys):
        return lax.dot_general(q, keys, (((1,), (1,)), ((), ())), preferred_element_type=F32)

    assert ATTN_PATTERNS[0][1] == 1 and all(d > 1 for _, d in ATTN_PATTERNS[1:])
    for g in reversed(range(N_GROUPS_ATTN)):
        dilation = ATTN_PATTERNS[g][1]
        q_ref, k_ref, v_ref = qkv[g]
        nb = seq // dilation // blk

        def per_block(idx, carry, g=g, dilation=dilation, q_ref=q_ref, k_ref=k_ref,
                      v_ref=v_ref, nb=nb):
            r, n = idx // nb, idx % nb
            row0 = pl.multiple_of(idx * blk, blk)
            prev0 = pl.multiple_of(jnp.maximum(row0 - blk, 0), blk)
            keys = jnp.concatenate([k_ref[pl.ds(prev0, blk), :],
                                    k_ref[pl.ds(row0, blk), :]], axis=0)
            vals = jnp.concatenate([v_ref[pl.ds(prev0, blk), :],
                                    v_ref[pl.ds(row0, blk), :]], axis=0)
            s = scores(q_ref[pl.ds(row0, blk), :], keys) + bias_ref[g, jnp.minimum(n, 1)]
            finish(g, dilation, r + dilation * n * blk, s, vals)
            return carry

        lax.fori_loop(0, dilation * nb, per_block, 0, unroll=ATTN_UNROLL)


def dilated_attention(planes, bias, batch, seq):
    hpg = HEADS_PER_GROUP

    def plane_spec(offset):
        return pl.BlockSpec((None, seq, LANES), lambda bi, h: (offset + h, bi, 0))

    in_specs, args = [], []
    for g in range(N_GROUPS_ATTN):
        for part in range(3):
            in_specs.append(plane_spec(part * hpg))
            args.append(planes[g])
    in_specs.append(plane_spec(3 * hpg))
    args.append(planes[0])
    in_specs.append(pl.BlockSpec((N_GROUPS_ATTN, None, 2, ATTN_BLOCK, 2 * ATTN_BLOCK),
                                 lambda bi, h: (0, h, 0, 0, 0)))
    args.append(bias)
    return pl.pallas_call(
        _attn_kernel,
        grid=(batch, hpg),
        in_specs=in_specs,
        out_specs=pl.BlockSpec((seq, LANES), lambda bi, h: (bi, h)),
        out_shape=jax.ShapeDtypeStruct((batch * seq, D_ATTN), BF16),
        scratch_shapes=[pltpu.VMEM((N_GROUPS_ATTN - 1, seq, LANES), F32),
                        pltpu.VMEM((N_GROUPS_ATTN - 1, seq, LANES), F32)],
        compiler_params=_cparams("parallel", "parallel"),
        name="dilated_attention",
    )(*args)


def _outproj_ln_kernel(y_ref, w_ref, x_ref, g_ref, b_ref, *refs, alpha, nt, nk):
    o_refs, accs = refs[:-2], refs[-2:]
    i, k = pl.program_id(0), pl.program_id(1)
    rc = x_ref.shape[0]
    d_model = w_ref.shape[1]
    n_ln = rc // LN_ROWS
    n_mm = n_ln if d_model % (n_ln * 2 * LANES) == 0 else 1
    cols = d_model // n_mm

    def matmul_part(acc_ref, first, c, anchor=None):
        sl = slice(c * cols, (c + 1) * cols)
        y = y_ref[...]
        if anchor is not None:
            tile = (BF16_ROWS, LANES)
            zero = (anchor[:tile[0], :tile[1]] * 0.0).astype(BF16)
            top = jnp.concatenate([y[:tile[0], :tile[1]] + zero, y[:tile[0], tile[1]:]], axis=1)
            y = jnp.concatenate([top, y[tile[0]:, :]], axis=0)
        d = jnp.dot(y, w_ref[:, sl], preferred_element_type=F32)
        if first:
            acc_ref[:, sl] = d
        else:
            acc_ref[:, sl] += d

    def layer_norm_part(done_ref, j):
        sub = pl.ds(j * LN_ROWS, LN_ROWS)
        src = pl.ds(pl.multiple_of(k * rc + j * LN_ROWS, LN_ROWS), LN_ROWS)
        v = done_ref[src, :] + alpha * x_ref[sub, :]
        mu = jnp.mean(v, axis=-1, keepdims=True)
        c = v - mu
        var = jnp.mean(c * c, axis=-1, keepdims=True)
        out = c * lax.rsqrt(var + LN_EPS) * g_ref[...] + b_ref[...]
        o_refs[0][sub, :] = out
        if len(o_refs) > 1:
            o_refs[1][sub, :] = out.astype(BF16)
        return out

    for parity in (0, 1):
        acc_ref, done_ref = accs[parity], accs[1 - parity]
        mine = (i % 2 == parity)
        for first in (True, False)[:min(nk, 2)]:
            cond = mine & ((k == 0) if first else (k > 0))

            if parity == 0:
                @pl.when(cond & (i == 0))
                def _(acc_ref=acc_ref, first=first):
                    for c in range(n_mm):
                        matmul_part(acc_ref, first, c)

            @pl.when(cond & (i > 0) & (i < nt))
            def _(acc_ref=acc_ref, done_ref=done_ref, first=first):
                matmul_part(acc_ref, first, 0)
                anchor = layer_norm_part(done_ref, 0)
                for step in range(1, max(n_ln, n_mm)):
                    if step < n_ln:
                        anchor = layer_norm_part(done_ref, step)
                    if step < n_mm:
                        matmul_part(acc_ref, first, step, anchor)

        if parity == nt % 2:
            @pl.when(i == nt)
            def _(done_ref=done_ref):
                for j in range(n_ln):
                    layer_norm_part(done_ref, j)


def outproj_layernorm(y, w, x_res, gamma, beta, alpha, bm, bk, emit_bf16):
    m, kdim = y.shape
    d_model = w.shape[1]
    nt, nk = m // bm, kdim // bk
    rc = bm // nk
    w_mode = pl.Buffered(1) if nk == 1 else None

    def chunk(i, k):
        return (jnp.where(i == 0, 0, (i - 1) * nk + k), 0)

    out_shape = [jax.ShapeDtypeStruct((m, d_model), F32)]
    out_specs = [pl.BlockSpec((rc, d_model), chunk)]
    if emit_bf16:
        out_shape.append(jax.ShapeDtypeStruct((m, d_model), BF16))
        out_specs.append(pl.BlockSpec((rc, d_model), chunk))
    return pl.pallas_call(
        functools.partial(_outproj_ln_kernel, alpha=alpha, nt=nt, nk=nk),
        grid=(nt + 1, nk),
        in_specs=[
            pl.BlockSpec((bm, bk), lambda i, k: (jnp.minimum(i, nt - 1), k)),
            pl.BlockSpec((bk, d_model), lambda i, k: (jnp.where(i == nt, 0, k), 0),
                         pipeline_mode=w_mode),
            pl.BlockSpec((rc, d_model), chunk),
            pl.BlockSpec((1, d_model), lambda i, k: (0, 0)),
            pl.BlockSpec((1, d_model), lambda i, k: (0, 0)),
        ],
        out_specs=out_specs,
        out_shape=out_shape,
        scratch_shapes=[pltpu.VMEM((bm, d_model), F32), pltpu.VMEM((bm, d_model), F32)],
        compiler_params=_cparams("arbitrary", "arbitrary"),
        name="outproj_layernorm",
    )(y, w, x_res, gamma.reshape(1, d_model).astype(F32), beta.reshape(1, d_model).astype(F32))


def _dt_kernel(x_ref, w_ref, bias_ref, a_ref, dt_ref, da_ref):
    w = w_ref[...]
    w_hi = w.astype(BF16)
    w_lo = (w - w_hi.astype(F32)).astype(BF16)
    x = x_ref[...]
    raw = (jnp.dot(x, w_hi, preferred_element_type=F32)
           + jnp.dot(x, w_lo, preferred_element_type=F32))
    v = raw + bias_ref[...]
    dt = jnp.maximum(v, 0.0) + jnp.log1p(jnp.exp(-jnp.abs(v)))
    dt_ref[...] = dt.T
    da_ref[...] = (dt * a_ref[...]).T


def dt_projection(x, w_in, dt_bias, a_log, bm):
    m, d_model = x.shape
    nh = dt_bias.shape[0]
    dt_block = w_in.shape[1] // nh - 1
    a_row = (-jnp.exp(a_log.astype(F32))).reshape(1, nh)
    return pl.pallas_call(
        _dt_kernel,
        grid=(m // bm,),
        in_specs=[
            pl.BlockSpec((bm, d_model), lambda i: (i, 0)),
            pl.BlockSpec((d_model, nh), lambda i: (0, dt_block)),
            pl.BlockSpec((1, nh), lambda i: (0, 0)),
            pl.BlockSpec((1, nh), lambda i: (0, 0)),
        ],
        out_specs=[pl.BlockSpec((nh, bm), lambda i: (0, i)),
                   pl.BlockSpec((nh, bm), lambda i: (0, i))],
        out_shape=[jax.ShapeDtypeStruct((nh, m), F32), jax.ShapeDtypeStruct((nh, m), F32)],
        compiler_params=_cparams("parallel"),
        name="dt_projection",
    )(x, w_in, dt_bias.astype(F32).reshape(1, nh), a_row)


DT_FLOOR = 1e-37


def _split3(v):
    hi = v.astype(BF16)
    rest = v - hi.astype(F32)
    mid = rest.astype(BF16)
    lo = (rest - mid.astype(F32)).astype(BF16)
    return hi, mid, lo


def _ssd_kernel(xs_ref, b_ref, c_ref, z_ref, dt_ref, da_ref,
                wx_ref, wb_ref, wc_ref, cbx_ref, cbb_ref, cbc_ref,
                dskip_ref, normw_ref, expand_ref, o_ref,
                state_sc, carry_ref, *, nsub):
    L = CHUNK
    nh = HEADS_PER_SSM_GROUP
    hp = lax.Precision.HIGHEST
    nt_dims = (((1,), (1,)), ((), ()))

    @pl.when(pl.program_id(2) == 0)
    def _():
        state_sc[...] = jnp.zeros_like(state_sc)
        carry_ref[...] = jnp.zeros_like(carry_ref)

    row_i = lax.broadcasted_iota(jnp.int32, (L, L), 0)
    col_i = lax.broadcasted_iota(jnp.int32, (L, L), 1)
    causal = row_i >= col_i
    triu = (row_i <= col_i).astype(F32)
    eye = (row_i == col_i).astype(F32)
    eye_b = eye.astype(BF16)
    lane = lax.broadcasted_iota(jnp.int32, (L, LANES), 1)
    head_lo = (lane < SSM_HEAD_DIM).astype(F32).astype(BF16)
    head_hi = (lane >= SSM_HEAD_DIM).astype(F32).astype(BF16)

    shift_mat = jnp.concatenate(
        [(col_i == row_i - s).astype(F32).astype(BF16) for s in range(1, CONV_WIDTH)],
        axis=0)
    conv_w = jnp.concatenate([wx_ref[...], wb_ref[...], wc_ref[...]], axis=1)
    conv_bias = jnp.concatenate([cbx_ref[...], cbb_ref[...], cbc_ref[...]], axis=1)
    tail_rows = carry_ref.shape[0]
    tail_row = lax.broadcasted_iota(jnp.int32, carry_ref.shape, 0)

    def conv_silu(cur):
        cur_f = cur.astype(F32)
        tail = carry_ref[...]
        carry_ref[...] = cur_f[L - tail_rows:, :]
        shifted = jnp.dot(shift_mat, cur, preferred_element_type=F32)
        acc = conv_bias + conv_w[CONV_WIDTH - 1:CONV_WIDTH, :] * cur_f
        head = jnp.zeros_like(tail)
        for s in range(1, CONV_WIDTH):
            k = CONV_WIDTH - 1 - s
            acc = acc + conv_w[k:k + 1, :] * shifted[(s - 1) * L:s * L, :]
            head = head + conv_w[k:k + 1, :] * jnp.where(tail_row < s,
                                                          pltpu.roll(tail, s, 0), 0.0)
        acc = jnp.concatenate([acc[:tail_rows] + head, acc[tail_rows:]], axis=0)
        return _silu(acc)

    def per_chunk(j, carry):
        rows = pl.ds(pl.multiple_of(j * L, L), L)
        xbc = conv_silu(jnp.concatenate([xs_ref[rows, :], b_ref[rows, :], c_ref[rows, :]],
                                        axis=1))
        xs = xbc[:, :GROUP_CH]
        bm = xbc[:, GROUP_CH:GROUP_CH + D_STATE]
        cm = xbc[:, GROUP_CH + D_STATE:]
        xs_b, bm_b, cm_b = xs.astype(BF16), bm.astype(BF16), cm.astype(BF16)

        dt_t = dt_ref[:, rows]
        a_t = jnp.dot(da_ref[:, rows], triu, precision=hp,
                      preferred_element_type=F32) * LOG2E
        log2dt = jnp.log(jnp.maximum(dt_t, DT_FLOOR)) * LOG2E
        a_last = a_t[:, L - 1:L]
        r_t = a_t - log2dt
        e_t = jnp.exp2(a_t)
        w_t = jnp.exp2(a_last - r_t)
        a_cols = lax.dot_general(eye, a_t, nt_dims, precision=hp,
                                 preferred_element_type=F32)
        parts = _split3(jnp.concatenate([e_t, w_t], axis=0))
        parts = jnp.concatenate(parts + (jnp.zeros_like(parts[0]),), axis=0)
        cols3 = lax.dot_general(eye_b, parts, nt_dims,
                                preferred_element_type=F32).astype(BF16)
        ew = jnp.dot(cols3, expand_ref[...], preferred_element_type=F32)
        e_exp, w_exp = ew[:, :GROUP_CH], ew[:, GROUP_CH:]

        cb = lax.dot_general(cm_b, bm_b, nt_dims, preferred_element_type=F32)
        state = state_sc[...]
        z_off = jnp.dot(cm_b, state.astype(BF16), preferred_element_type=F32)

        ys = []
        for c in range(PAIRS):
            lanes = slice(c * LANES, (c + 1) * LANES)
            x_c = xs_b[:, lanes]
            rhs = jnp.concatenate([x_c * head_lo, x_c * head_hi], axis=0)
            lhs = []
            for h in (2 * c, 2 * c + 1):
                seg = a_cols[:, h:h + 1] - r_t[h:h + 1, :]
                lhs.append((jnp.exp2(jnp.where(causal, seg, -jnp.inf)) * cb).astype(BF16))
            y_c = jnp.dot(jnp.concatenate(lhs, axis=1), rhs, preferred_element_type=F32)
            ys.append(y_c + e_exp[:, lanes] * z_off[:, lanes])
        y = jnp.concatenate(ys, axis=1)

        upd = lax.dot_general(bm_b, (xs * w_exp).astype(BF16),
                              (((0,), (0,)), ((), ())), preferred_element_type=F32)
        state_sc[...] = state * e_exp[L - 1:L, :] + upd

        y = y + dskip_ref[...] * xs
        y = y * _silu(z_ref[rows, :].astype(F32))
        ms = jnp.mean(y * y, axis=-1, keepdims=True)
        y = y * lax.rsqrt(ms + RMS_EPS) * normw_ref[...]
        o_ref[rows, :] = y.astype(o_ref.dtype)
        return carry

    lax.fori_loop(0, nsub, per_chunk, 0, unroll=SSD_CHUNK_UNROLL)


def ssd_mixer_core(proj, dt_t, da_t, conv_w, conv_b, d_skip, norm_w, batch, seq, rows_per_step):
    m = proj.shape[0]
    d_inner = SSM_GROUPS * GROUP_CH
    t = rows_per_step
    nsteps = seq // t
    xcol0 = d_inner // GROUP_CH
    bcol0 = 2 * d_inner // D_STATE
    ccol0 = bcol0 + SSM_GROUPS
    wb0 = d_inner // D_STATE
    wc0 = wb0 + SSM_GROUPS
    conv_w = conv_w.astype(F32)
    conv_b = conv_b.astype(F32).reshape(1, -1)
    d_exp = jnp.repeat(d_skip.astype(F32), SSM_HEAD_DIM).reshape(1, d_inner)
    norm_w = norm_w.astype(F32).reshape(1, d_inner)
    head_exp = jnp.repeat(jnp.eye(HEADS_PER_SSM_GROUP, dtype=F32), SSM_HEAD_DIM, axis=1)
    zeros = jnp.zeros_like(head_exp)
    pair = jnp.block([[head_exp, zeros], [zeros, head_exp]])
    expand = jnp.concatenate([pair, pair, pair, jnp.zeros_like(pair)], axis=0).astype(BF16)

    def row(bi, s):
        return bi * nsteps + s

    in_specs = [
        pl.BlockSpec((t, GROUP_CH), lambda bi, g, s: (row(bi, s), xcol0 + g)),
        pl.BlockSpec((t, D_STATE), lambda bi, g, s: (row(bi, s), bcol0 + g)),
        pl.BlockSpec((t, D_STATE), lambda bi, g, s: (row(bi, s), ccol0 + g)),
        pl.BlockSpec((t, GROUP_CH), lambda bi, g, s: (row(bi, s), g)),
        pl.BlockSpec((HEADS_PER_SSM_GROUP, t), lambda bi, g, s: (g, row(bi, s))),
        pl.BlockSpec((HEADS_PER_SSM_GROUP, t), lambda bi, g, s: (g, row(bi, s))),
        pl.BlockSpec((CONV_WIDTH, GROUP_CH), lambda bi, g, s: (0, g)),
        pl.BlockSpec((CONV_WIDTH, D_STATE), lambda bi, g, s: (0, wb0 + g)),
        pl.BlockSpec((CONV_WIDTH, D_STATE), lambda bi, g, s: (0, wc0 + g)),
        pl.BlockSpec((1, GROUP_CH), lambda bi, g, s: (0, g)),
        pl.BlockSpec((1, D_STATE), lambda bi, g, s: (0, wb0 + g)),
        pl.BlockSpec((1, D_STATE), lambda bi, g, s: (0, wc0 + g)),
        pl.BlockSpec((1, GROUP_CH), lambda bi, g, s: (0, g)),
        pl.BlockSpec((1, GROUP_CH), lambda bi, g, s: (0, g)),
        pl.BlockSpec(expand.shape, lambda bi, g, s: (0, 0)),
    ]
    return pl.pallas_call(
        functools.partial(_ssd_kernel, nsub=t // CHUNK),
        grid=(batch, SSM_GROUPS, nsteps),
        in_specs=in_specs,
        out_specs=pl.BlockSpec((t, GROUP_CH), lambda bi, g, s: (row(bi, s), g)),
        out_shape=jax.ShapeDtypeStruct((m, d_inner), BF16),
        scratch_shapes=[pltpu.VMEM((D_STATE, GROUP_CH), F32),
                        pltpu.VMEM((CONV_TAIL_ROWS, GROUP_CH + 2 * D_STATE), F32)],
        compiler_params=_cparams("parallel", "parallel", "arbitrary"),
        name="ssd_mixer_core",
    )(proj, proj, proj, proj, dt_t, da_t, conv_w, conv_w, conv_w, conv_b, conv_b, conv_b,
      d_exp, norm_w, expand)


def _attention_layer(x, w_in, w_out, rel_bias, gamma, beta, alpha, emit_bf16):
    batch, seq, d_model = x.shape
    dilations = tuple(d for _, d in ATTN_PATTERNS)
    group_cols = 3 * D_ATTN
    gate_col0 = N_GROUPS_ATTN * group_cols
    xperm = cast_permute(x, dilations, bm=CAST_ROWS)
    planes = []
    bm, bn = PROJ_TILE
    for g in range(N_GROUPS_ATTN):
        first = g * group_cols // bn
        if g == 0:
            nqkv, gate_first = group_cols // bn, gate_col0 // bn
            n = group_cols + D_ATTN
            col_block = lambda j, nqkv=nqkv, gate_first=gate_first: jnp.where(
                j < nqkv, j, j - nqkv + gate_first)
        else:
            n = group_cols
            col_block = lambda j, first=first: first + j
        planes.append(proj_planes(xperm[g], w_in.astype(F32), col_block, n, D_ATTN,
                                  HEAD_DIM ** -0.5 * LOG2E, bm, bn))
    y = dilated_attention(planes, masked_bias(rel_bias), batch, seq)
    return outproj_layernorm(y, w_out.astype(BF16), x.reshape(batch * seq, d_model),
                             gamma, beta, alpha, *OUTPROJ_TILE_ATTN, emit_bf16=emit_bf16)


def _ssd_layer(x, xb, w_in, conv_w, conv_b, dt_bias, a_log, d_skip, norm_w, w_out,
               gamma, beta, alpha):
    batch, seq, d_model = x.shape
    m = batch * seq
    n_main = w_in.shape[1] - dt_bias.shape[0]
    xf = x.reshape(m, d_model)
    proj = proj_flat(xb.reshape(m, d_model), w_in.astype(F32), n_main, *PROJ_TILE)
    dt_t, da_t = dt_projection(xb.reshape(m, d_model), w_in, dt_bias, a_log, bm=DT_ROWS)
    y = ssd_mixer_core(proj, dt_t, da_t, conv_w, conv_b, d_skip, norm_w, batch, seq,
                       rows_per_step=min(SSD_ROWS, seq))
    return outproj_layernorm(y, w_out.astype(BF16), xf, gamma, beta, alpha,
                             *OUTPROJ_TILE_SSM, emit_bf16=False)


def kernel(x, w_in_attn, w_out_attn, rel_bias, w_in_ssm, conv_w, conv_b, dt_bias, a_log,
           d_skip, ssm_norm_w, w_out_ssm, ln_g, ln_b):
    batch, seq, d_model = x.shape
    depth = ln_g.shape[0]
    alpha = (2 * depth) ** 0.25
    xb = None
    for i in range(depth):
        j = i // 2
        if i % 2 == 0:
            outs = _attention_layer(x, w_in_attn[j], w_out_attn[j], rel_bias,
                                    ln_g[i], ln_b[i], alpha, emit_bf16=i + 1 < depth)
            if i + 1 < depth:
                xb = outs[1].reshape(batch, seq, d_model)
        else:
            outs = _ssd_layer(x, xb, w_in_ssm[j], conv_w[j], conv_b[j], dt_bias[j], a_log[j],
                              d_skip[j], ssm_norm_w[j], w_out_ssm[j], ln_g[i], ln_b[i], alpha)
        x = outs[0].reshape(batch, seq, d_model)
    return x
```

```python
import functools
import math

import jax
import jax.numpy as jnp
from jax import lax
from jax.experimental import pallas as pl
from jax.experimental.pallas import tpu as pltpu

F32 = jnp.float32
BF16 = jnp.bfloat16

LANES = 128
BF16_ROWS = 16
VMEM_LIMIT_BYTES = 56 * 1024 * 1024

ATTN_PATTERNS = ((128, 1), (512, 4), (2048, 16))
N_GROUPS_ATTN = 3
HEADS_PER_GROUP = 16
HEAD_DIM = 128
D_ATTN = HEADS_PER_GROUP * HEAD_DIM
ATTN_BLOCK = 128
NUM_BUCKETS = 32
MAX_DISTANCE = 2048
SSM_HEAD_DIM = 64
SSM_GROUPS = 8
HEADS_PER_SSM_GROUP = 16
D_STATE = 128
CONV_WIDTH = 4
CHUNK = 128
LN_EPS = 1e-5
RMS_EPS = 1e-5
NEG_INF = -1e30
LOG2E = 1.4426950408889634

CAST_ROWS = 256
PROJ_TILE = (2048, 512)
OUTPROJ_W_BUFFERS = 3
ATTN_UNROLL = 32
OUTPROJ_TILE_ATTN = (256, 2048)
OUTPROJ_TILE_SSM = (512, 1024)
LN_ROWS = 32
DT_ROWS = 1024
SSD_ROWS = 2048
SSD_CHUNK_UNROLL = 2

GROUP_CH = HEADS_PER_SSM_GROUP * SSM_HEAD_DIM
PAIRS = GROUP_CH // LANES
CONV_TAIL_ROWS = 8


def _cparams(*sem):
    return pltpu.CompilerParams(dimension_semantics=sem, vmem_limit_bytes=VMEM_LIMIT_BYTES)


def _cast_permute_kernel(x_ref, *o_refs, dilations):
    xb = x_ref[...].astype(BF16)
    bm = xb.shape[0]
    dst = lax.broadcasted_iota(jnp.int32, (bm, bm), 0)
    src = lax.broadcasted_iota(jnp.int32, (bm, bm), 1)
    for o_ref, d in zip(o_refs, dilations):
        rows = bm // d
        if d == 1:
            o_ref[0] = xb
            continue
        perm = (src == (dst % rows) * d + dst // rows).astype(BF16)
        xp = jnp.dot(perm, xb, preferred_element_type=F32).astype(BF16)
        for r in range(d):
            o_ref[r] = xp[r * rows:(r + 1) * rows, :]


def cast_permute(x, dilations, bm):
    b, s, d_model = x.shape
    return pl.pallas_call(
        functools.partial(_cast_permute_kernel, dilations=dilations),
        grid=(b, s // bm),
        in_specs=[pl.BlockSpec((None, bm, d_model), lambda bi, i: (bi, i, 0))],
        out_specs=[pl.BlockSpec((None, d, bm // d, d_model), lambda bi, i: (bi, 0, i, 0))
                   for d in dilations],
        out_shape=[jax.ShapeDtypeStruct((b, d, s // d, d_model), BF16) for d in dilations],
        compiler_params=_cparams("parallel", "parallel"),
        name="cast_permute",
    )(x)


def _silu(v):
    h = 0.5 * v
    return h + h * jnp.tanh(h)


def _proj_kernel(x_ref, w_ref, o_ref, *, q_blocks, q_scale):
    x = x_ref[...].reshape(-1, x_ref.shape[-1])
    acc = jnp.dot(x, w_ref[...].astype(BF16), preferred_element_type=F32)
    if q_blocks:
        acc = acc * jnp.where(pl.program_id(1) < q_blocks, q_scale, 1.0)
    if len(o_ref.shape) == 3:
        for c in range(o_ref.shape[0]):
            o_ref[c] = acc[:, c * LANES:(c + 1) * LANES].astype(o_ref.dtype)
    else:
        o_ref[...] = acc.astype(o_ref.dtype)


def proj_planes(xp, w, col_block, n, q_cols, q_scale, bm, bn):
    b, dilation, sub_len, d_model = xp.shape
    bm = min(bm, dilation * sub_len)
    tiles = dilation * sub_len // bm
    if sub_len >= bm:
        per_class = sub_len // bm
        x_spec = pl.BlockSpec(
            (None, 1, bm, d_model),
            lambda t, j: (t // tiles, (t % tiles) // per_class, (t % tiles) % per_class, 0),
            pipeline_mode=pl.Buffered(1))
    else:
        x_spec = pl.BlockSpec((None, bm // sub_len, sub_len, d_model),
                              lambda t, j: (t // tiles, t % tiles, 0, 0),
                              pipeline_mode=pl.Buffered(1))
    return pl.pallas_call(
        functools.partial(_proj_kernel, q_blocks=q_cols // bn, q_scale=q_scale),
        grid=(b * tiles, n // bn),
        in_specs=[x_spec, pl.BlockSpec((d_model, bn), lambda t, j: (0, col_block(j)))],
        out_specs=pl.BlockSpec((bn // LANES, bm, LANES), lambda t, j: (j, t, 0)),
        out_shape=jax.ShapeDtypeStruct((n // LANES, b * dilation * sub_len, LANES), BF16),
        compiler_params=_cparams("parallel", "arbitrary"),
        name=f"proj_planes_d{dilation}",
    )(xp, w)


def proj_flat(xb, w, n_out, bm, bn):
    m, d_model = xb.shape
    bm = min(bm, m)
    return pl.pallas_call(
        functools.partial(_proj_kernel, q_blocks=0, q_scale=1.0),
        grid=(m // bm, n_out // bn),
        in_specs=[
            pl.BlockSpec((bm, d_model), lambda i, j: (i, 0), pipeline_mode=pl.Buffered(1)),
            pl.BlockSpec((d_model, bn), lambda i, j: (0, j)),
        ],
        out_specs=pl.BlockSpec((bm, bn), lambda i, j: (i, j)),
        out_shape=jax.ShapeDtypeStruct((m, n_out), BF16),
        compiler_params=_cparams("parallel", "arbitrary"),
        name="proj_flat",
    )(xb, w)


def _t5_causal_bucket(dist):
    max_exact = NUM_BUCKETS // 2
    d_f = jnp.maximum(dist, 1).astype(F32)
    large = max_exact + (jnp.log(d_f / max_exact) / math.log(MAX_DISTANCE / max_exact)
                         * (NUM_BUCKETS - max_exact)).astype(jnp.int32)
    large = jnp.minimum(large, NUM_BUCKETS - 1)
    return jnp.where(dist < max_exact, dist, large)


def _band_buckets():
    qi = jnp.arange(ATTN_BLOCK)[:, None]
    ki = jnp.arange(2 * ATTN_BLOCK)[None, :]
    delta = ATTN_BLOCK + qi - ki
    out = []
    for window, dilation in ATTN_PATTERNS:
        span = window // dilation
        band = (delta >= 0) & (delta <= span)
        bucket = _t5_causal_bucket(jnp.clip(delta, 0, None) * dilation)
        out.append(jnp.where(band, bucket, -1))
    return jnp.stack(out).astype(jnp.int32)


def _bias_kernel(table_ref, bucket_ref, o_ref):
    g = pl.program_id(0)
    bucket = bucket_ref[...]
    prev_half = lax.broadcasted_iota(jnp.int32, bucket.shape, 1) < ATTN_BLOCK

    def per_head(h, carry):
        acc = jnp.full(bucket.shape, NEG_INF, F32)
        for b in range(NUM_BUCKETS):
            acc = jnp.where(bucket == b, table_ref[b, g * HEADS_PER_GROUP + h] * LOG2E, acc)
        o_ref[h, 0] = jnp.where(prev_half, NEG_INF, acc)
        o_ref[h, 1] = acc
        return carry

    lax.fori_loop(0, HEADS_PER_GROUP, per_head, 0)


def masked_bias(rel_bias):
    buckets = _band_buckets()
    return pl.pallas_call(
        _bias_kernel,
        grid=(N_GROUPS_ATTN,),
        in_specs=[
            pl.BlockSpec(memory_space=pltpu.SMEM),
            pl.BlockSpec((None, ATTN_BLOCK, 2 * ATTN_BLOCK), lambda g: (g, 0, 0)),
        ],
        out_specs=pl.BlockSpec((None, HEADS_PER_GROUP, 2, ATTN_BLOCK, 2 * ATTN_BLOCK),
                               lambda g: (g, 0, 0, 0, 0)),
        out_shape=jax.ShapeDtypeStruct(
            (N_GROUPS_ATTN, HEADS_PER_GROUP, 2, ATTN_BLOCK, 2 * ATTN_BLOCK), F32),
        compiler_params=_cparams("parallel"),
        name="masked_bias",
    )(rel_bias.astype(F32), buckets)


def _attn_kernel(q0, k0, v0, q1, k1, v1, q2, k2, v2, gate_ref, bias_ref, y_ref, o_sc, lse_sc):
    seq = y_ref.shape[0]
    blk = ATTN_BLOCK
    qkv = ((q0, k0, v0), (q1, k1, v1), (q2, k2, v2))

    def finish(g, dilation, tok0, s, vals):
        m = jnp.max(s, axis=-1, keepdims=True)
        p = jnp.exp2(s - m)
        denom = jnp.sum(p, axis=-1, keepdims=True)
        acc = jnp.dot(p.astype(BF16), vals, preferred_element_type=F32)
        o = acc * (1.0 / denom)
        lse = jnp.broadcast_to(m + jnp.log2(denom), (blk, LANES))
        if dilation > 1:
            rows = pl.ds(tok0, blk, stride=dilation)
            o_sc[g - 1, rows, :] = o
            lse_sc[g - 1, rows, :] = lse
            return
        rows = pl.ds(tok0, blk)
        l1, l2 = lse_sc[0, rows, :], lse_sc[1, rows, :]
        top = jnp.maximum(jnp.maximum(lse, l1), l2)
        w0, w1, w2 = jnp.exp2(lse - top), jnp.exp2(l1 - top), jnp.exp2(l2 - top)
        o = (w0 * o + w1 * o_sc[0, rows, :] + w2 * o_sc[1, rows, :]) * (1.0 / (w0 + w1 + w2))
        gate = gate_ref[rows, :].astype(F32)
        y_ref[rows, :] = (o * _silu(gate)).astype(y_ref.dtype)

    def scores(q, keys):
        return lax.dot_general(q, keys, (((1,), (1,)), ((), ())), preferred_element_type=F32)

    assert ATTN_PATTERNS[0][1] == 1 and all(d > 1 for _, d in ATTN_PATTERNS[1:])
    for g in reversed(range(N_GROUPS_ATTN)):
        dilation = ATTN_PATTERNS[g][1]
        q_ref, k_ref, v_ref = qkv[g]
        nb = seq // dilation // blk

        def per_block(idx, carry, g=g, dilation=dilation, q_ref=q_ref, k_ref=k_ref,
                      v_ref=v_ref, nb=nb):
            r, n = idx // nb, idx % nb
            row0 = pl.multiple_of(idx * blk, blk)
            prev0 = pl.multiple_of(jnp.maximum(row0 - blk, 0), blk)
            keys = jnp.concatenate([k_ref[pl.ds(prev0, blk), :],
                                    k_ref[pl.ds(row0, blk), :]], axis=0)
            vals = jnp.concatenate([v_ref[pl.ds(prev0, blk), :],
                                    v_ref[pl.ds(row0, blk), :]], axis=0)
            s = scores(q_ref[pl.ds(row0, blk), :], keys) + bias_ref[g, jnp.minimum(n, 1)]
            finish(g, dilation, r + dilation * n * blk, s, vals)
            return carry

        lax.fori_loop(0, dilation * nb, per_block, 0, unroll=ATTN_UNROLL)


def dilated_attention(planes, bias, batch, seq):
    hpg = HEADS_PER_GROUP

    def plane_spec(offset):
        return pl.BlockSpec((None, seq, LANES), lambda bi, h: (offset + h, bi, 0))

    in_specs, args = [], []
    for g in range(N_GROUPS_ATTN):
        for part in range(3):
            in_specs.append(plane_spec(part * hpg))
            args.append(planes[g])
    in_specs.append(plane_spec(3 * hpg))
    args.append(planes[0])
    in_specs.append(pl.BlockSpec((N_GROUPS_ATTN, None, 2, ATTN_BLOCK, 2 * ATTN_BLOCK),
                                 lambda bi, h: (0, h, 0, 0, 0)))
    args.append(bias)
    return pl.pallas_call(
        _attn_kernel,
        grid=(batch, hpg),
        in_specs=in_specs,
        out_specs=pl.BlockSpec((seq, LANES), lambda bi, h: (bi, h)),
        out_shape=jax.ShapeDtypeStruct((batch * seq, D_ATTN), BF16),
        scratch_shapes=[pltpu.VMEM((N_GROUPS_ATTN - 1, seq, LANES), F32),
                        pltpu.VMEM((N_GROUPS_ATTN - 1, seq, LANES), F32)],
        compiler_params=_cparams("parallel", "parallel"),
        name="dilated_attention",
    )(*args)


def _outproj_ln_kernel(y_ref, w_ref, x_ref, g_ref, b_ref, *refs, alpha, nt, nk):
    i, k = pl.program_id(0), pl.program_id(1)
    rc = x_ref.shape[0]
    d_model = w_ref.shape[1]
    if nk == 1:
        o_refs, accs = refs[:-2], refs[-2:]
        w_now = w_ref
    else:
        o_refs, accs, (wbuf, wsem) = refs[:-4], refs[-4:-2], refs[-2:]
        bk = wbuf.shape[1]
        step = i * nk + k
        n_matmul_steps = nt * nk

        def w_copy(s):
            kblk = pl.multiple_of((s % nk) * bk, bk)
            slot = s % OUTPROJ_W_BUFFERS
            return pltpu.make_async_copy(w_ref.at[pl.ds(kblk, bk), :], wbuf.at[slot],
                                         wsem.at[slot])

        @pl.when(step == 0)
        def _():
            for s in range(OUTPROJ_W_BUFFERS - 1):
                w_copy(s).start()

        ahead = step + OUTPROJ_W_BUFFERS - 1

        @pl.when(ahead < n_matmul_steps)
        def _():
            w_copy(ahead).start()

        @pl.when(step < n_matmul_steps)
        def _():
            w_copy(step).wait()

        w_now = wbuf.at[step % OUTPROJ_W_BUFFERS]
    n_ln = rc // LN_ROWS
    n_mm = n_ln if d_model % (n_ln * 2 * LANES) == 0 else 1
    cols = d_model // n_mm

    def matmul_part(acc_ref, first, c, anchor=None):
        sl = slice(c * cols, (c + 1) * cols)
        y = y_ref[...]
        if anchor is not None:
            tile = (BF16_ROWS, LANES)
            zero = (anchor[:tile[0], :tile[1]] * 0.0).astype(BF16)
            top = jnp.concatenate([y[:tile[0], :tile[1]] + zero, y[:tile[0], tile[1]:]], axis=1)
            y = jnp.concatenate([top, y[tile[0]:, :]], axis=0)
        d = jnp.dot(y, w_now[:, sl], preferred_element_type=F32)
        if first:
            acc_ref[:, sl] = d
        else:
            acc_ref[:, sl] += d

    def layer_norm_part(done_ref, j):
        sub = pl.ds(j * LN_ROWS, LN_ROWS)
        src = pl.ds(pl.multiple_of(k * rc + j * LN_ROWS, LN_ROWS), LN_ROWS)
        v = done_ref[src, :] + alpha * x_ref[sub, :]
        mu = jnp.mean(v, axis=-1, keepdims=True)
        c = v - mu
        var = jnp.mean(c * c, axis=-1, keepdims=True)
        out = c * lax.rsqrt(var + LN_EPS) * g_ref[...] + b_ref[...]
        o_refs[0][sub, :] = out
        if len(o_refs) > 1:
            o_refs[1][sub, :] = out.astype(BF16)
        return out

    for parity in (0, 1):
        acc_ref, done_ref = accs[parity], accs[1 - parity]
        mine = (i % 2 == parity)
        for first in (True, False)[:min(nk, 2)]:
            cond = mine & ((k == 0) if first else (k > 0))

            if parity == 0:
                @pl.when(cond & (i == 0))
                def _(acc_ref=acc_ref, first=first):
                    for c in range(n_mm):
                        matmul_part(acc_ref, first, c)

            @pl.when(cond & (i > 0) & (i < nt))
            def _(acc_ref=acc_ref, done_ref=done_ref, first=first):
                matmul_part(acc_ref, first, 0)
                anchor = layer_norm_part(done_ref, 0)
                for step in range(1, max(n_ln, n_mm)):
                    if step < n_ln:
                        anchor = layer_norm_part(done_ref, step)
                    if step < n_mm:
                        matmul_part(acc_ref, first, step, anchor)

        if parity == nt % 2:
            @pl.when(i == nt)
            def _(done_ref=done_ref):
                for j in range(n_ln):
                    layer_norm_part(done_ref, j)


def outproj_layernorm(y, w, x_res, gamma, beta, alpha, bm, bk, emit_bf16):
    m, kdim = y.shape
    d_model = w.shape[1]
    nt, nk = m // bm, kdim // bk
    rc = bm // nk
    scratch = [pltpu.VMEM((bm, d_model), F32), pltpu.VMEM((bm, d_model), F32)]
    if nk == 1:
        w_spec = pl.BlockSpec((bk, d_model), lambda i, k: (0, 0), pipeline_mode=pl.Buffered(1))
    else:
        w_spec = pl.BlockSpec(memory_space=pl.ANY)
        scratch += [pltpu.VMEM((OUTPROJ_W_BUFFERS, bk, d_model), BF16),
                    pltpu.SemaphoreType.DMA((OUTPROJ_W_BUFFERS,))]

    def chunk(i, k):
        return (jnp.where(i == 0, 0, (i - 1) * nk + k), 0)

    out_shape = [jax.ShapeDtypeStruct((m, d_model), F32)]
    out_specs = [pl.BlockSpec((rc, d_model), chunk)]
    if emit_bf16:
        out_shape.append(jax.ShapeDtypeStruct((m, d_model), BF16))
        out_specs.append(pl.BlockSpec((rc, d_model), chunk))
    return pl.pallas_call(
        functools.partial(_outproj_ln_kernel, alpha=alpha, nt=nt, nk=nk),
        grid=(nt + 1, nk),
        in_specs=[
            pl.BlockSpec((bm, bk), lambda i, k: (jnp.minimum(i, nt - 1), k)),
            w_spec,
            pl.BlockSpec((rc, d_model), chunk),
            pl.BlockSpec((1, d_model), lambda i, k: (0, 0)),
            pl.BlockSpec((1, d_model), lambda i, k: (0, 0)),
        ],
        out_specs=out_specs,
        out_shape=out_shape,
        scratch_shapes=scratch,
        compiler_params=_cparams("arbitrary", "arbitrary"),
        name="outproj_layernorm",
    )(y, w, x_res, gamma.reshape(1, d_model).astype(F32), beta.reshape(1, d_model).astype(F32))


def _dt_kernel(x_ref, w_ref, bias_ref, a_ref, dt_ref, da_ref):
    w = w_ref[...]
    w_hi = w.astype(BF16)
    w_lo = (w - w_hi.astype(F32)).astype(BF16)
    x = x_ref[...]
    raw = (jnp.dot(x, w_hi, preferred_element_type=F32)
           + jnp.dot(x, w_lo, preferred_element_type=F32))
    v = raw + bias_ref[...]
    dt = jnp.maximum(v, 0.0) + jnp.log1p(jnp.exp(-jnp.abs(v)))
    dt_ref[...] = dt.T
    da_ref[...] = (dt * a_ref[...]).T


def dt_projection(x, w_in, dt_bias, a_log, bm):
    m, d_model = x.shape
    nh = dt_bias.shape[0]
    dt_block = w_in.shape[1] // nh - 1
    a_row = (-jnp.exp(a_log.astype(F32))).reshape(1, nh)
    return pl.pallas_call(
        _dt_kernel,
        grid=(m // bm,),
        in_specs=[
            pl.BlockSpec((bm, d_model), lambda i: (i, 0)),
            pl.BlockSpec((d_model, nh), lambda i: (0, dt_block)),
            pl.BlockSpec((1, nh), lambda i: (0, 0)),
            pl.BlockSpec((1, nh), lambda i: (0, 0)),
        ],
        out_specs=[pl.BlockSpec((nh, bm), lambda i: (0, i)),
                   pl.BlockSpec((nh, bm), lambda i: (0, i))],
        out_shape=[jax.ShapeDtypeStruct((nh, m), F32), jax.ShapeDtypeStruct((nh, m), F32)],
        compiler_params=_cparams("parallel"),
        name="dt_projection",
    )(x, w_in, dt_bias.astype(F32).reshape(1, nh), a_row)


DT_FLOOR = 1e-37


def _split3(v):
    hi = v.astype(BF16)
    rest = v - hi.astype(F32)
    mid = rest.astype(BF16)
    lo = (rest - mid.astype(F32)).astype(BF16)
    return hi, mid, lo


def _ssd_kernel(xs_ref, b_ref, c_ref, z_ref, dt_ref, da_ref,
                wx_ref, wb_ref, wc_ref, cbx_ref, cbb_ref, cbc_ref,
                dskip_ref, normw_ref, expand_ref, o_ref,
                state_sc, carry_ref, *, nsub):
    L = CHUNK
    nh = HEADS_PER_SSM_GROUP
    hp = lax.Precision.HIGHEST
    nt_dims = (((1,), (1,)), ((), ()))

    @pl.when(pl.program_id(2) == 0)
    def _():
        state_sc[...] = jnp.zeros_like(state_sc)
        carry_ref[...] = jnp.zeros_like(carry_ref)

    row_i = lax.broadcasted_iota(jnp.int32, (L, L), 0)
    col_i = lax.broadcasted_iota(jnp.int32, (L, L), 1)
    causal = row_i >= col_i
    triu = (row_i <= col_i).astype(F32)
    eye = (row_i == col_i).astype(F32)
    eye_b = eye.astype(BF16)
    lane = lax.broadcasted_iota(jnp.int32, (L, LANES), 1)
    head_lo = (lane < SSM_HEAD_DIM).astype(F32).astype(BF16)
    head_hi = (lane >= SSM_HEAD_DIM).astype(F32).astype(BF16)

    shift_mat = jnp.concatenate(
        [(col_i == row_i - s).astype(F32).astype(BF16) for s in range(1, CONV_WIDTH)],
        axis=0)
    conv_w = jnp.concatenate([wx_ref[...], wb_ref[...], wc_ref[...]], axis=1)
    conv_bias = jnp.concatenate([cbx_ref[...], cbb_ref[...], cbc_ref[...]], axis=1)
    tail_rows = carry_ref.shape[0]
    tail_row = lax.broadcasted_iota(jnp.int32, carry_ref.shape, 0)

    def conv_silu(cur):
        cur_f = cur.astype(F32)
        tail = carry_ref[...]
        carry_ref[...] = cur_f[L - tail_rows:, :]
        shifted = jnp.dot(shift_mat, cur, preferred_element_type=F32)
        acc = conv_bias + conv_w[CONV_WIDTH - 1:CONV_WIDTH, :] * cur_f
        head = jnp.zeros_like(tail)
        for s in range(1, CONV_WIDTH):
            k = CONV_WIDTH - 1 - s
            acc = acc + conv_w[k:k + 1, :] * shifted[(s - 1) * L:s * L, :]
            head = head + conv_w[k:k + 1, :] * jnp.where(tail_row < s,
                                                          pltpu.roll(tail, s, 0), 0.0)
        acc = jnp.concatenate([acc[:tail_rows] + head, acc[tail_rows:]], axis=0)
        return _silu(acc)

    def per_chunk(j, carry):
        rows = pl.ds(pl.multiple_of(j * L, L), L)
        xbc = conv_silu(jnp.concatenate([xs_ref[rows, :], b_ref[rows, :], c_ref[rows, :]],
                                        axis=1))
        xs = xbc[:, :GROUP_CH]
        bm = xbc[:, GROUP_CH:GROUP_CH + D_STATE]
        cm = xbc[:, GROUP_CH + D_STATE:]
        xs_b, bm_b, cm_b = xs.astype(BF16), bm.astype(BF16), cm.astype(BF16)

        dt_t = dt_ref[:, rows]
        a_t = jnp.dot(da_ref[:, rows], triu, precision=hp,
                      preferred_element_type=F32) * LOG2E
        log2dt = jnp.log(jnp.maximum(dt_t, DT_FLOOR)) * LOG2E
        a_last = a_t[:, L - 1:L]
        r_t = a_t - log2dt
        e_t = jnp.exp2(a_t)
        w_t = jnp.exp2(a_last - r_t)
        a_cols = lax.dot_general(eye, a_t, nt_dims, precision=hp,
                                 preferred_element_type=F32)
        parts = _split3(jnp.concatenate([e_t, w_t], axis=0))
        parts = jnp.concatenate(parts + (jnp.zeros_like(parts[0]),), axis=0)
        cols3 = lax.dot_general(eye_b, parts, nt_dims,
                                preferred_element_type=F32).astype(BF16)
        ew = jnp.dot(cols3, expand_ref[...], preferred_element_type=F32)
        e_exp, w_exp = ew[:, :GROUP_CH], ew[:, GROUP_CH:]

        cb = lax.dot_general(cm_b, bm_b, nt_dims, preferred_element_type=F32)
        state = state_sc[...]
        z_off = jnp.dot(cm_b, state.astype(BF16), preferred_element_type=F32)

        ys = []
        for c in range(PAIRS):
            lanes = slice(c * LANES, (c + 1) * LANES)
            x_c = xs_b[:, lanes]
            rhs = jnp.concatenate([x_c * head_lo, x_c * head_hi], axis=0)
            lhs = []
            for h in (2 * c, 2 * c + 1):
                seg = a_cols[:, h:h + 1] - r_t[h:h + 1, :]
                lhs.append((jnp.exp2(jnp.where(causal, seg, -jnp.inf)) * cb).astype(BF16))
            y_c = jnp.dot(jnp.concatenate(lhs, axis=1), rhs, preferred_element_type=F32)
            ys.append(y_c + e_exp[:, lanes] * z_off[:, lanes])
        y = jnp.concatenate(ys, axis=1)

        upd = lax.dot_general(bm_b, (xs * w_exp).astype(BF16),
                              (((0,), (0,)), ((), ())), preferred_element_type=F32)
        state_sc[...] = state * e_exp[L - 1:L, :] + upd

        y = y + dskip_ref[...] * xs
        y = y * _silu(z_ref[rows, :].astype(F32))
        ms = jnp.mean(y * y, axis=-1, keepdims=True)
        y = y * lax.rsqrt(ms + RMS_EPS) * normw_ref[...]
        o_ref[rows, :] = y.astype(o_ref.dtype)
        return carry

    lax.fori_loop(0, nsub, per_chunk, 0, unroll=SSD_CHUNK_UNROLL)


def ssd_mixer_core(proj, dt_t, da_t, conv_w, conv_b, d_skip, norm_w, batch, seq, rows_per_step):
    m = proj.shape[0]
    d_inner = SSM_GROUPS * GROUP_CH
    t = rows_per_step
    nsteps = seq // t
    xcol0 = d_inner // GROUP_CH
    bcol0 = 2 * d_inner // D_STATE
    ccol0 = bcol0 + SSM_GROUPS
    wb0 = d_inner // D_STATE
    wc0 = wb0 + SSM_GROUPS
    conv_w = conv_w.astype(F32)
    conv_b = conv_b.astype(F32).reshape(1, -1)
    d_exp = jnp.repeat(d_skip.astype(F32), SSM_HEAD_DIM).reshape(1, d_inner)
    norm_w = norm_w.astype(F32).reshape(1, d_inner)
    head_exp = jnp.repeat(jnp.eye(HEADS_PER_SSM_GROUP, dtype=F32), SSM_HEAD_DIM, axis=1)
    zeros = jnp.zeros_like(head_exp)
    pair = jnp.block([[head_exp, zeros], [zeros, head_exp]])
    expand = jnp.concatenate([pair, pair, pair, jnp.zeros_like(pair)], axis=0).astype(BF16)

    def row(bi, s):
        return bi * nsteps + s

    in_specs = [
        pl.BlockSpec((t, GROUP_CH), lambda bi, g, s: (row(bi, s), xcol0 + g)),
        pl.BlockSpec((t, D_STATE), lambda bi, g, s: (row(bi, s), bcol0 + g)),
        pl.BlockSpec((t, D_STATE), lambda bi, g, s: (row(bi, s), ccol0 + g)),
        pl.BlockSpec((t, GROUP_CH), lambda bi, g, s: (row(bi, s), g)),
        pl.BlockSpec((HEADS_PER_SSM_GROUP, t), lambda bi, g, s: (g, row(bi, s))),
        pl.BlockSpec((HEADS_PER_SSM_GROUP, t), lambda bi, g, s: (g, row(bi, s))),
        pl.BlockSpec((CONV_WIDTH, GROUP_CH), lambda bi, g, s: (0, g)),
        pl.BlockSpec((CONV_WIDTH, D_STATE), lambda bi, g, s: (0, wb0 + g)),
        pl.BlockSpec((CONV_WIDTH, D_STATE), lambda bi, g, s: (0, wc0 + g)),
        pl.BlockSpec((1, GROUP_CH), lambda bi, g, s: (0, g)),
        pl.BlockSpec((1, D_STATE), lambda bi, g, s: (0, wb0 + g)),
        pl.BlockSpec((1, D_STATE), lambda bi, g, s: (0, wc0 + g)),
        pl.BlockSpec((1, GROUP_CH), lambda bi, g, s: (0, g)),
        pl.BlockSpec((1, GROUP_CH), lambda bi, g, s: (0, g)),
        pl.BlockSpec(expand.shape, lambda bi, g, s: (0, 0)),
    ]
    return pl.pallas_call(
        functools.partial(_ssd_kernel, nsub=t // CHUNK),
        grid=(batch, SSM_GROUPS, nsteps),
        in_specs=in_specs,
        out_specs=pl.BlockSpec((t, GROUP_CH), lambda bi, g, s: (row(bi, s), g)),
        out_shape=jax.ShapeDtypeStruct((m, d_inner), BF16),
        scratch_shapes=[pltpu.VMEM((D_STATE, GROUP_CH), F32),
                        pltpu.VMEM((CONV_TAIL_ROWS, GROUP_CH + 2 * D_STATE), F32)],
        compiler_params=_cparams("parallel", "parallel", "arbitrary"),
        name="ssd_mixer_core",
    )(proj, proj, proj, proj, dt_t, da_t, conv_w, conv_w, conv_w, conv_b, conv_b, conv_b,
      d_exp, norm_w, expand)


def _attention_layer(x, w_in, w_out, rel_bias, gamma, beta, alpha, emit_bf16):
    batch, seq, d_model = x.shape
    dilations = tuple(d for _, d in ATTN_PATTERNS)
    group_cols = 3 * D_ATTN
    gate_col0 = N_GROUPS_ATTN * group_cols
    xperm = cast_permute(x, dilations, bm=CAST_ROWS)
    planes = []
    bm, bn = PROJ_TILE
    for g in range(N_GROUPS_ATTN):
        first = g * group_cols // bn
        if g == 0:
            nqkv, gate_first = group_cols // bn, gate_col0 // bn
            n = group_cols + D_ATTN
            col_block = lambda j, nqkv=nqkv, gate_first=gate_first: jnp.where(
                j < nqkv, j, j - nqkv + gate_first)
        else:
            n = group_cols
            col_block = lambda j, first=first: first + j
        planes.append(proj_planes(xperm[g], w_in.astype(F32), col_block, n, D_ATTN,
                                  HEAD_DIM ** -0.5 * LOG2E, bm, bn))
    y = dilated_attention(planes, masked_bias(rel_bias), batch, seq)
    return outproj_layernorm(y, w_out.astype(BF16), x.reshape(batch * seq, d_model),
                             gamma, beta, alpha, *OUTPROJ_TILE_ATTN, emit_bf16=emit_bf16)


def _ssd_layer(x, xb, w_in, conv_w, conv_b, dt_bias, a_log, d_skip, norm_w, w_out,
               gamma, beta, alpha):
    batch, seq, d_model = x.shape
    m = batch * seq
    n_main = w_in.shape[1] - dt_bias.shape[0]
    xf = x.reshape(m, d_model)
    proj = proj_flat(xb.reshape(m, d_model), w_in.astype(F32), n_main, *PROJ_TILE)
    dt_t, da_t = dt_projection(xb.reshape(m, d_model), w_in, dt_bias, a_log, bm=DT_ROWS)
    y = ssd_mixer_core(proj, dt_t, da_t, conv_w, conv_b, d_skip, norm_w, batch, seq,
                       rows_per_step=min(SSD_ROWS, seq))
    return outproj_layernorm(y, w_out.astype(BF16), xf, gamma, beta, alpha,
                             *OUTPROJ_TILE_SSM, emit_bf16=False)


def kernel(x, w_in_attn, w_out_attn, rel_bias, w_in_ssm, conv_w, conv_b, dt_bias, a_log,
           d_skip, ssm_norm_w, w_out_ssm, ln_g, ln_b):
    batch, seq, d_model = x.shape
    depth = ln_g.shape[0]
    alpha = (2 * depth) ** 0.25
    xb = None
    for i in range(depth):
        j = i // 2
        if i % 2 == 0:
            outs = _attention_layer(x, w_in_attn[j], w_out_attn[j], rel_bias,
                                    ln_g[i], ln_b[i], alpha, emit_bf16=i + 1 < depth)
            if i + 1 < depth:
                xb = outs[1].reshape(batch, seq, d_model)
        else:
            outs = _ssd_layer(x, xb, w_in_ssm[j], conv_w[j], conv_b[j], dt_bias[j], a_log[j],
                              d_skip[j], ssm_norm_w[j], w_out_ssm[j], ln_g[i], ln_b[i], alpha)
        x = outs[0].reshape(batch, seq, d_model)
    return x
```

```python
import functools
import math

import jax
import jax.numpy as jnp
from jax import lax
from jax.experimental import pallas as pl
from jax.experimental.pallas import tpu as pltpu

F32 = jnp.float32
BF16 = jnp.bfloat16

LANES = 128
BF16_ROWS = 16
VMEM_LIMIT_BYTES = 60 * 1024 * 1024

ATTN_PATTERNS = ((128, 1), (512, 4), (2048, 16))
N_GROUPS_ATTN = 3
HEADS_PER_GROUP = 16
HEAD_DIM = 128
D_ATTN = HEADS_PER_GROUP * HEAD_DIM
ATTN_BLOCK = 128
NUM_BUCKETS = 32
MAX_DISTANCE = 2048
SSM_HEAD_DIM = 64
SSM_GROUPS = 8
HEADS_PER_SSM_GROUP = 16
D_STATE = 128
CONV_WIDTH = 4
CHUNK = 128
LN_EPS = 1e-5
RMS_EPS = 1e-5
NEG_INF = -1e30
LOG2E = 1.4426950408889634

CAST_ROWS = 256
PROJ_TILE = (2048, 512)
ATTN_UNROLL = 32
OUTPROJ_TILE_ATTN = (256, 2048)
OUTPROJ_TILE_SSM = (512, 1024)
LN_ROWS = 32
DT_PIECES = 4
SSD_ROWS = 2048
SSD_CHUNK_UNROLL = 2

GROUP_CH = HEADS_PER_SSM_GROUP * SSM_HEAD_DIM
PAIRS = GROUP_CH // LANES
CONV_TAIL_ROWS = 8


def _cparams(*sem):
    return pltpu.CompilerParams(dimension_semantics=sem, vmem_limit_bytes=VMEM_LIMIT_BYTES)


def _cast_permute_kernel(x_ref, *o_refs, dilations):
    xb = x_ref[...].astype(BF16)
    bm = xb.shape[0]
    dst = lax.broadcasted_iota(jnp.int32, (bm, bm), 0)
    src = lax.broadcasted_iota(jnp.int32, (bm, bm), 1)
    for o_ref, d in zip(o_refs, dilations):
        rows = bm // d
        if d == 1:
            o_ref[0] = xb
            continue
        perm = (src == (dst % rows) * d + dst // rows).astype(BF16)
        xp = jnp.dot(perm, xb, preferred_element_type=F32).astype(BF16)
        for r in range(d):
            o_ref[r] = xp[r * rows:(r + 1) * rows, :]


def cast_permute(x, dilations, bm):
    b, s, d_model = x.shape
    return pl.pallas_call(
        functools.partial(_cast_permute_kernel, dilations=dilations),
        grid=(b, s // bm),
        in_specs=[pl.BlockSpec((None, bm, d_model), lambda bi, i: (bi, i, 0))],
        out_specs=[pl.BlockSpec((None, d, bm // d, d_model), lambda bi, i: (bi, 0, i, 0))
                   for d in dilations],
        out_shape=[jax.ShapeDtypeStruct((b, d, s // d, d_model), BF16) for d in dilations],
        compiler_params=_cparams("parallel", "parallel"),
        name="cast_permute",
    )(x)


def _silu(v):
    h = 0.5 * v
    return h + h * jnp.tanh(h)


def _proj_kernel(x_ref, w_ref, o_ref, *, q_blocks, q_scale):
    x = x_ref[...].reshape(-1, x_ref.shape[-1])
    acc = jnp.dot(x, w_ref[...].astype(BF16), preferred_element_type=F32)
    if q_blocks:
        acc = acc * jnp.where(pl.program_id(1) < q_blocks, q_scale, 1.0)
    if len(o_ref.shape) == 3:
        for c in range(o_ref.shape[0]):
            o_ref[c] = acc[:, c * LANES:(c + 1) * LANES].astype(o_ref.dtype)
    else:
        o_ref[...] = acc.astype(o_ref.dtype)


def proj_planes(xp, w, col_block, n, q_cols, q_scale, bm, bn):
    b, dilation, sub_len, d_model = xp.shape
    bm = min(bm, dilation * sub_len)
    tiles = dilation * sub_len // bm
    if sub_len >= bm:
        per_class = sub_len // bm
        x_spec = pl.BlockSpec(
            (None, 1, bm, d_model),
            lambda t, j: (t // tiles, (t % tiles) // per_class, (t % tiles) % per_class, 0),
            pipeline_mode=pl.Buffered(1))
    else:
        x_spec = pl.BlockSpec((None, bm // sub_len, sub_len, d_model),
                              lambda t, j: (t // tiles, t % tiles, 0, 0),
                              pipeline_mode=pl.Buffered(1))
    return pl.pallas_call(
        functools.partial(_proj_kernel, q_blocks=q_cols // bn, q_scale=q_scale),
        grid=(b * tiles, n // bn),
        in_specs=[x_spec, pl.BlockSpec((d_model, bn), lambda t, j: (0, col_block(j)))],
        out_specs=pl.BlockSpec((bn // LANES, bm, LANES), lambda t, j: (j, t, 0)),
        out_shape=jax.ShapeDtypeStruct((n // LANES, b * dilation * sub_len, LANES), BF16),
        compiler_params=_cparams("parallel", "arbitrary"),
        name=f"proj_planes_d{dilation}",
    )(xp, w)


def _t5_causal_bucket(dist):
    max_exact = NUM_BUCKETS // 2
    d_f = jnp.maximum(dist, 1).astype(F32)
    large = max_exact + (jnp.log(d_f / max_exact) / math.log(MAX_DISTANCE / max_exact)
                         * (NUM_BUCKETS - max_exact)).astype(jnp.int32)
    large = jnp.minimum(large, NUM_BUCKETS - 1)
    return jnp.where(dist < max_exact, dist, large)


def _band_buckets():
    qi = jnp.arange(ATTN_BLOCK)[:, None]
    ki = jnp.arange(2 * ATTN_BLOCK)[None, :]
    delta = ATTN_BLOCK + qi - ki
    out = []
    for window, dilation in ATTN_PATTERNS:
        span = window // dilation
        band = (delta >= 0) & (delta <= span)
        bucket = _t5_causal_bucket(jnp.clip(delta, 0, None) * dilation)
        out.append(jnp.where(band, bucket, -1))
    return jnp.stack(out).astype(jnp.int32)


def _bias_kernel(table_ref, bucket_ref, o_ref):
    g = pl.program_id(0)
    bucket = bucket_ref[...]
    prev_half = lax.broadcasted_iota(jnp.int32, bucket.shape, 1) < ATTN_BLOCK

    def per_head(h, carry):
        acc = jnp.full(bucket.shape, NEG_INF, F32)
        for b in range(NUM_BUCKETS):
            acc = jnp.where(bucket == b, table_ref[b, g * HEADS_PER_GROUP + h] * LOG2E, acc)
        o_ref[h, 0] = jnp.where(prev_half, NEG_INF, acc)
        o_ref[h, 1] = acc
        return carry

    lax.fori_loop(0, HEADS_PER_GROUP, per_head, 0)


def masked_bias(rel_bias):
    buckets = _band_buckets()
    return pl.pallas_call(
        _bias_kernel,
        grid=(N_GROUPS_ATTN,),
        in_specs=[
            pl.BlockSpec(memory_space=pltpu.SMEM),
            pl.BlockSpec((None, ATTN_BLOCK, 2 * ATTN_BLOCK), lambda g: (g, 0, 0)),
        ],
        out_specs=pl.BlockSpec((None, HEADS_PER_GROUP, 2, ATTN_BLOCK, 2 * ATTN_BLOCK),
                               lambda g: (g, 0, 0, 0, 0)),
        out_shape=jax.ShapeDtypeStruct(
            (N_GROUPS_ATTN, HEADS_PER_GROUP, 2, ATTN_BLOCK, 2 * ATTN_BLOCK), F32),
        compiler_params=_cparams("parallel"),
        name="masked_bias",
    )(rel_bias.astype(F32), buckets)


def _attn_kernel(q0, k0, v0, q1, k1, v1, q2, k2, v2, gate_ref, bias_ref, y_ref, o_sc, lse_sc):
    seq = y_ref.shape[0]
    blk = ATTN_BLOCK
    qkv = ((q0, k0, v0), (q1, k1, v1), (q2, k2, v2))

    def finish(g, dilation, tok0, s, vals):
        m = jnp.max(s, axis=-1, keepdims=True)
        p = jnp.exp2(s - m)
        denom = jnp.sum(p, axis=-1, keepdims=True)
        acc = jnp.dot(p.astype(BF16), vals, preferred_element_type=F32)
        o = acc * (1.0 / denom)
        lse = jnp.broadcast_to(m + jnp.log2(denom), (blk, LANES))
        if dilation > 1:
            rows = pl.ds(tok0, blk, stride=dilation)
            o_sc[g - 1, rows, :] = o
            lse_sc[g - 1, rows, :] = lse
            return
        rows = pl.ds(tok0, blk)
        l1, l2 = lse_sc[0, rows, :], lse_sc[1, rows, :]
        top = jnp.maximum(jnp.maximum(lse, l1), l2)
        w0, w1, w2 = jnp.exp2(lse - top), jnp.exp2(l1 - top), jnp.exp2(l2 - top)
        o = (w0 * o + w1 * o_sc[0, rows, :] + w2 * o_sc[1, rows, :]) * (1.0 / (w0 + w1 + w2))
        gate = gate_ref[rows, :].astype(F32)
        y_ref[rows, :] = (o * _silu(gate)).astype(y_ref.dtype)

    def scores(q, keys):
        return lax.dot_general(q, keys, (((1,), (1,)), ((), ())), preferred_element_type=F32)

    assert ATTN_PATTERNS[0][1] == 1 and all(d > 1 for _, d in ATTN_PATTERNS[1:])
    for g in reversed(range(N_GROUPS_ATTN)):
        dilation = ATTN_PATTERNS[g][1]
        q_ref, k_ref, v_ref = qkv[g]
        nb = seq // dilation // blk

        def per_block(idx, carry, g=g, dilation=dilation, q_ref=q_ref, k_ref=k_ref,
                      v_ref=v_ref, nb=nb):
            r, n = idx // nb, idx % nb
            row0 = pl.multiple_of(idx * blk, blk)
            prev0 = pl.multiple_of(jnp.maximum(row0 - blk, 0), blk)
            keys = jnp.concatenate([k_ref[pl.ds(prev0, blk), :],
                                    k_ref[pl.ds(row0, blk), :]], axis=0)
            vals = jnp.concatenate([v_ref[pl.ds(prev0, blk), :],
                                    v_ref[pl.ds(row0, blk), :]], axis=0)
            s = scores(q_ref[pl.ds(row0, blk), :], keys) + bias_ref[g, jnp.minimum(n, 1)]
            finish(g, dilation, r + dilation * n * blk, s, vals)
            return carry

        lax.fori_loop(0, dilation * nb, per_block, 0, unroll=ATTN_UNROLL)


def dilated_attention(planes, bias, batch, seq):
    hpg = HEADS_PER_GROUP

    def plane_spec(offset):
        return pl.BlockSpec((None, seq, LANES), lambda bi, h: (offset + h, bi, 0))

    in_specs, args = [], []
    for g in range(N_GROUPS_ATTN):
        for part in range(3):
            in_specs.append(plane_spec(part * hpg))
            args.append(planes[g])
    in_specs.append(plane_spec(3 * hpg))
    args.append(planes[0])
    in_specs.append(pl.BlockSpec((N_GROUPS_ATTN, None, 2, ATTN_BLOCK, 2 * ATTN_BLOCK),
                                 lambda bi, h: (0, h, 0, 0, 0)))
    args.append(bias)
    return pl.pallas_call(
        _attn_kernel,
        grid=(batch, hpg),
        in_specs=in_specs,
        out_specs=pl.BlockSpec((seq, LANES), lambda bi, h: (bi, h)),
        out_shape=jax.ShapeDtypeStruct((batch * seq, D_ATTN), BF16),
        scratch_shapes=[pltpu.VMEM((N_GROUPS_ATTN - 1, seq, LANES), F32),
                        pltpu.VMEM((N_GROUPS_ATTN - 1, seq, LANES), F32)],
        compiler_params=_cparams("parallel", "parallel"),
        name="dilated_attention",
    )(*args)


def _outproj_ln_kernel(y_ref, w_ref, x_ref, g_ref, b_ref, *refs, alpha, nt, nk):
    o_refs, accs = refs[:-2], refs[-2:]
    i, k = pl.program_id(0), pl.program_id(1)
    rc = x_ref.shape[0]
    d_model = w_ref.shape[1]
    n_ln = rc // LN_ROWS
    n_mm = n_ln if d_model % (n_ln * 2 * LANES) == 0 else 1
    cols = d_model // n_mm

    def matmul_part(acc_ref, first, c, anchor=None):
        sl = slice(c * cols, (c + 1) * cols)
        y = y_ref[...]
        if anchor is not None:
            tile = (BF16_ROWS, LANES)
            zero = (anchor[:tile[0], :tile[1]] * 0.0).astype(BF16)
            top = jnp.concatenate([y[:tile[0], :tile[1]] + zero, y[:tile[0], tile[1]:]], axis=1)
            y = jnp.concatenate([top, y[tile[0]:, :]], axis=0)
        d = jnp.dot(y, w_ref[:, sl], preferred_element_type=F32)
        if first:
            acc_ref[:, sl] = d
        else:
            acc_ref[:, sl] += d

    def layer_norm_part(done_ref, j):
        sub = pl.ds(j * LN_ROWS, LN_ROWS)
        src = pl.ds(pl.multiple_of(k * rc + j * LN_ROWS, LN_ROWS), LN_ROWS)
        v = done_ref[src, :] + alpha * x_ref[sub, :]
        mu = jnp.mean(v, axis=-1, keepdims=True)
        c = v - mu
        var = jnp.mean(c * c, axis=-1, keepdims=True)
        out = c * lax.rsqrt(var + LN_EPS) * g_ref[...] + b_ref[...]
        o_refs[0][sub, :] = out
        if len(o_refs) > 1:
            o_refs[1][sub, :] = out.astype(BF16)
        return out

    for parity in (0, 1):
        acc_ref, done_ref = accs[parity], accs[1 - parity]
        mine = (i % 2 == parity)
        for first in (True, False)[:min(nk, 2)]:
            cond = mine & ((k == 0) if first else (k > 0))

            if parity == 0:
                @pl.when(cond & (i == 0))
                def _(acc_ref=acc_ref, first=first):
                    for c in range(n_mm):
                        matmul_part(acc_ref, first, c)

            @pl.when(cond & (i > 0) & (i < nt))
            def _(acc_ref=acc_ref, done_ref=done_ref, first=first):
                matmul_part(acc_ref, first, 0)
                anchor = layer_norm_part(done_ref, 0)
                for step in range(1, max(n_ln, n_mm)):
                    if step < n_ln:
                        anchor = layer_norm_part(done_ref, step)
                    if step < n_mm:
                        matmul_part(acc_ref, first, step, anchor)

        if parity == nt % 2:
            @pl.when(i == nt)
            def _(done_ref=done_ref):
                for j in range(n_ln):
                    layer_norm_part(done_ref, j)


def outproj_layernorm(y, w, x_res, gamma, beta, alpha, bm, bk, emit_bf16):
    m, kdim = y.shape
    d_model = w.shape[1]
    nt, nk = m // bm, kdim // bk
    rc = bm // nk
    w_mode = pl.Buffered(1) if nk == 1 else None

    def chunk(i, k):
        return (jnp.where(i == 0, 0, (i - 1) * nk + k), 0)

    out_shape = [jax.ShapeDtypeStruct((m, d_model), F32)]
    out_specs = [pl.BlockSpec((rc, d_model), chunk)]
    if emit_bf16:
        out_shape.append(jax.ShapeDtypeStruct((m, d_model), BF16))
        out_specs.append(pl.BlockSpec((rc, d_model), chunk))
    return pl.pallas_call(
        functools.partial(_outproj_ln_kernel, alpha=alpha, nt=nt, nk=nk),
        grid=(nt + 1, nk),
        in_specs=[
            pl.BlockSpec((bm, bk), lambda i, k: (jnp.minimum(i, nt - 1), k)),
            pl.BlockSpec((bk, d_model), lambda i, k: (jnp.where(i == nt, 0, k), 0),
                         pipeline_mode=w_mode),
            pl.BlockSpec((rc, d_model), chunk),
            pl.BlockSpec((1, d_model), lambda i, k: (0, 0)),
            pl.BlockSpec((1, d_model), lambda i, k: (0, 0)),
        ],
        out_specs=out_specs,
        out_shape=out_shape,
        scratch_shapes=[pltpu.VMEM((bm, d_model), F32), pltpu.VMEM((bm, d_model), F32)],
        compiler_params=_cparams("arbitrary", "arbitrary"),
        name="outproj_layernorm",
    )(y, w, x_res, gamma.reshape(1, d_model).astype(F32), beta.reshape(1, d_model).astype(F32))


def _dt_kernel(x_ref, w_ref, bias_ref, a_ref, dt_ref, da_ref):
    w = w_ref[...]
    w_hi = w.astype(BF16)
    w_lo = (w - w_hi.astype(F32)).astype(BF16)
    x = x_ref[...]
    raw = (jnp.dot(x, w_hi, preferred_element_type=F32)
           + jnp.dot(x, w_lo, preferred_element_type=F32))
    v = raw + bias_ref[...]
    dt = jnp.maximum(v, 0.0) + jnp.log1p(jnp.exp(-jnp.abs(v)))
    dt_ref[...] = dt.T
    da_ref[...] = (dt * a_ref[...]).T


def _proj_dt_kernel(x_ref, w_ref, wdt_ref, bias_ref, a_ref, o_ref, dt_ref, da_ref):
    _proj_kernel(x_ref, w_ref, o_ref, q_blocks=0, q_scale=1.0)

    @pl.when(pl.program_id(1) == 0)
    def _():
        rows = x_ref.shape[0] // DT_PIECES
        for p in range(DT_PIECES):
            part = pl.ds(p * rows, rows)
            _dt_kernel(x_ref.at[part, :], wdt_ref, bias_ref, a_ref,
                       dt_ref.at[:, part], da_ref.at[:, part])


def proj_flat_dt(xb, w_in, dt_bias, a_log, bm, bn):
    m, d_model = xb.shape
    nh = dt_bias.shape[0]
    n_out = w_in.shape[1] - nh
    dt_block = w_in.shape[1] // nh - 1
    bm = min(bm, m)
    a_row = (-jnp.exp(a_log.astype(F32))).reshape(1, nh)
    return pl.pallas_call(
        _proj_dt_kernel,
        grid=(m // bm, n_out // bn),
        in_specs=[
            pl.BlockSpec((bm, d_model), lambda i, j: (i, 0), pipeline_mode=pl.Buffered(1)),
            pl.BlockSpec((d_model, bn), lambda i, j: (0, j)),
            pl.BlockSpec((d_model, nh), lambda i, j: (0, dt_block),
                         pipeline_mode=pl.Buffered(1)),
            pl.BlockSpec((1, nh), lambda i, j: (0, 0)),
            pl.BlockSpec((1, nh), lambda i, j: (0, 0)),
        ],
        out_specs=[pl.BlockSpec((bm, bn), lambda i, j: (i, j)),
                   pl.BlockSpec((nh, bm), lambda i, j: (0, i)),
                   pl.BlockSpec((nh, bm), lambda i, j: (0, i))],
        out_shape=[jax.ShapeDtypeStruct((m, n_out), BF16),
                   jax.ShapeDtypeStruct((nh, m), F32), jax.ShapeDtypeStruct((nh, m), F32)],
        compiler_params=_cparams("parallel", "arbitrary"),
        name="proj_flat_dt",
    )(xb, w_in, w_in, dt_bias.astype(F32).reshape(1, nh), a_row)


DT_FLOOR = 1e-37


def _split3(v):
    hi = v.astype(BF16)
    rest = v - hi.astype(F32)
    mid = rest.astype(BF16)
    lo = (rest - mid.astype(F32)).astype(BF16)
    return hi, mid, lo


def _ssd_kernel(xs_ref, b_ref, c_ref, z_ref, dt_ref, da_ref,
                wx_ref, wb_ref, wc_ref, cbx_ref, cbb_ref, cbc_ref,
                dskip_ref, normw_ref, expand_ref, o_ref,
                state_sc, carry_ref, *, nsub):
    L = CHUNK
    nh = HEADS_PER_SSM_GROUP
    hp = lax.Precision.HIGHEST
    nt_dims = (((1,), (1,)), ((), ()))

    @pl.when(pl.program_id(2) == 0)
    def _():
        state_sc[...] = jnp.zeros_like(state_sc)
        carry_ref[...] = jnp.zeros_like(carry_ref)

    row_i = lax.broadcasted_iota(jnp.int32, (L, L), 0)
    col_i = lax.broadcasted_iota(jnp.int32, (L, L), 1)
    causal = row_i >= col_i
    triu = (row_i <= col_i).astype(F32)
    eye = (row_i == col_i).astype(F32)
    eye_b = eye.astype(BF16)
    lane = lax.broadcasted_iota(jnp.int32, (L, LANES), 1)
    head_lo = (lane < SSM_HEAD_DIM).astype(F32).astype(BF16)
    head_hi = (lane >= SSM_HEAD_DIM).astype(F32).astype(BF16)

    shift_mat = jnp.concatenate(
        [(col_i == row_i - s).astype(F32).astype(BF16) for s in range(1, CONV_WIDTH)],
        axis=0)
    conv_w = jnp.concatenate([wx_ref[...], wb_ref[...], wc_ref[...]], axis=1)
    conv_bias = jnp.concatenate([cbx_ref[...], cbb_ref[...], cbc_ref[...]], axis=1)
    tail_rows = carry_ref.shape[0]
    tail_row = lax.broadcasted_iota(jnp.int32, carry_ref.shape, 0)

    def conv_silu(cur):
        cur_f = cur.astype(F32)
        tail = carry_ref[...]
        carry_ref[...] = cur_f[L - tail_rows:, :]
        shifted = jnp.dot(shift_mat, cur, preferred_element_type=F32)
        acc = conv_bias + conv_w[CONV_WIDTH - 1:CONV_WIDTH, :] * cur_f
        head = jnp.zeros_like(tail)
        for s in range(1, CONV_WIDTH):
            k = CONV_WIDTH - 1 - s
            acc = acc + conv_w[k:k + 1, :] * shifted[(s - 1) * L:s * L, :]
            head = head + conv_w[k:k + 1, :] * jnp.where(tail_row < s,
                                                          pltpu.roll(tail, s, 0), 0.0)
        acc = jnp.concatenate([acc[:tail_rows] + head, acc[tail_rows:]], axis=0)
        return _silu(acc)

    def per_chunk(j, carry):
        rows = pl.ds(pl.multiple_of(j * L, L), L)
        xbc = conv_silu(jnp.concatenate([xs_ref[rows, :], b_ref[rows, :], c_ref[rows, :]],
                                        axis=1))
        xs = xbc[:, :GROUP_CH]
        bm = xbc[:, GROUP_CH:GROUP_CH + D_STATE]
        cm = xbc[:, GROUP_CH + D_STATE:]
        xs_b, bm_b, cm_b = xs.astype(BF16), bm.astype(BF16), cm.astype(BF16)

        dt_t = dt_ref[:, rows]
        a_t = jnp.dot(da_ref[:, rows], triu, precision=hp,
                      preferred_element_type=F32) * LOG2E
        log2dt = jnp.log(jnp.maximum(dt_t, DT_FLOOR)) * LOG2E
        a_last = a_t[:, L - 1:L]
        r_t = a_t - log2dt
        e_t = jnp.exp2(a_t)
        w_t = jnp.exp2(a_last - r_t)
        a_cols = lax.dot_general(eye, a_t, nt_dims, precision=hp,
                                 preferred_element_type=F32)
        parts = _split3(jnp.concatenate([e_t, w_t], axis=0))
        parts = jnp.concatenate(parts + (jnp.zeros_like(parts[0]),), axis=0)
        cols3 = lax.dot_general(eye_b, parts, nt_dims,
                                preferred_element_type=F32).astype(BF16)
        ew = jnp.dot(cols3, expand_ref[...], preferred_element_type=F32)
        e_exp, w_exp = ew[:, :GROUP_CH], ew[:, GROUP_CH:]

        cb = lax.dot_general(cm_b, bm_b, nt_dims, preferred_element_type=F32)
        state = state_sc[...]
        z_off = jnp.dot(cm_b, state.astype(BF16), preferred_element_type=F32)

        ys = []
        for c in range(PAIRS):
            lanes = slice(c * LANES, (c + 1) * LANES)
            x_c = xs_b[:, lanes]
            rhs = jnp.concatenate([x_c * head_lo, x_c * head_hi], axis=0)
            lhs = []
            for h in (2 * c, 2 * c + 1):
                seg = a_cols[:, h:h + 1] - r_t[h:h + 1, :]
                lhs.append((jnp.exp2(jnp.where(causal, seg, -jnp.inf)) * cb).astype(BF16))
            y_c = jnp.dot(jnp.concatenate(lhs, axis=1), rhs, preferred_element_type=F32)
            ys.append(y_c + e_exp[:, lanes] * z_off[:, lanes])
        y = jnp.concatenate(ys, axis=1)

        upd = lax.dot_general(bm_b, (xs * w_exp).astype(BF16),
                              (((0,), (0,)), ((), ())), preferred_element_type=F32)
        state_sc[...] = state * e_exp[L - 1:L, :] + upd

        y = y + dskip_ref[...] * xs
        y = y * _silu(z_ref[rows, :].astype(F32))
        ms = jnp.mean(y * y, axis=-1, keepdims=True)
        y = y * lax.rsqrt(ms + RMS_EPS) * normw_ref[...]
        o_ref[rows, :] = y.astype(o_ref.dtype)
        return carry

    lax.fori_loop(0, nsub, per_chunk, 0, unroll=SSD_CHUNK_UNROLL)


def ssd_mixer_core(proj, dt_t, da_t, conv_w, conv_b, d_skip, norm_w, batch, seq, rows_per_step):
    m = proj.shape[0]
    d_inner = SSM_GROUPS * GROUP_CH
    t = rows_per_step
    nsteps = seq // t
    xcol0 = d_inner // GROUP_CH
    bcol0 = 2 * d_inner // D_STATE
    ccol0 = bcol0 + SSM_GROUPS
    wb0 = d_inner // D_STATE
    wc0 = wb0 + SSM_GROUPS
    conv_w = conv_w.astype(F32)
    conv_b = conv_b.astype(F32).reshape(1, -1)
    d_exp = jnp.repeat(d_skip.astype(F32), SSM_HEAD_DIM).reshape(1, d_inner)
    norm_w = norm_w.astype(F32).reshape(1, d_inner)
    head_exp = jnp.repeat(jnp.eye(HEADS_PER_SSM_GROUP, dtype=F32), SSM_HEAD_DIM, axis=1)
    zeros = jnp.zeros_like(head_exp)
    pair = jnp.block([[head_exp, zeros], [zeros, head_exp]])
    expand = jnp.concatenate([pair, pair, pair, jnp.zeros_like(pair)], axis=0).astype(BF16)

    def row(bi, s):
        return bi * nsteps + s

    in_specs = [
        pl.BlockSpec((t, GROUP_CH), lambda bi, g, s: (row(bi, s), xcol0 + g)),
        pl.BlockSpec((t, D_STATE), lambda bi, g, s: (row(bi, s), bcol0 + g)),
        pl.BlockSpec((t, D_STATE), lambda bi, g, s: (row(bi, s), ccol0 + g)),
        pl.BlockSpec((t, GROUP_CH), lambda bi, g, s: (row(bi, s), g)),
        pl.BlockSpec((HEADS_PER_SSM_GROUP, t), lambda bi, g, s: (g, row(bi, s))),
        pl.BlockSpec((HEADS_PER_SSM_GROUP, t), lambda bi, g, s: (g, row(bi, s))),
        pl.BlockSpec((CONV_WIDTH, GROUP_CH), lambda bi, g, s: (0, g)),
        pl.BlockSpec((CONV_WIDTH, D_STATE), lambda bi, g, s: (0, wb0 + g)),
        pl.BlockSpec((CONV_WIDTH, D_STATE), lambda bi, g, s: (0, wc0 + g)),
        pl.BlockSpec((1, GROUP_CH), lambda bi, g, s: (0, g)),
        pl.BlockSpec((1, D_STATE), lambda bi, g, s: (0, wb0 + g)),
        pl.BlockSpec((1, D_STATE), lambda bi, g, s: (0, wc0 + g)),
        pl.BlockSpec((1, GROUP_CH), lambda bi, g, s: (0, g)),
        pl.BlockSpec((1, GROUP_CH), lambda bi, g, s: (0, g)),
        pl.BlockSpec(expand.shape, lambda bi, g, s: (0, 0)),
    ]
    return pl.pallas_call(
        functools.partial(_ssd_kernel, nsub=t // CHUNK),
        grid=(batch, SSM_GROUPS, nsteps),
        in_specs=in_specs,
        out_specs=pl.BlockSpec((t, GROUP_CH), lambda bi, g, s: (row(bi, s), g)),
        out_shape=jax.ShapeDtypeStruct((m, d_inner), BF16),
        scratch_shapes=[pltpu.VMEM((D_STATE, GROUP_CH), F32),
                        pltpu.VMEM((CONV_TAIL_ROWS, GROUP_CH + 2 * D_STATE), F32)],
        compiler_params=_cparams("parallel", "parallel", "arbitrary"),
        name="ssd_mixer_core",
    )(proj, proj, proj, proj, dt_t, da_t, conv_w, conv_w, conv_w, conv_b, conv_b, conv_b,
      d_exp, norm_w, expand)


def _attention_layer(x, w_in, w_out, rel_bias, gamma, beta, alpha, emit_bf16):
    batch, seq, d_model = x.shape
    dilations = tuple(d for _, d in ATTN_PATTERNS)
    group_cols = 3 * D_ATTN
    gate_col0 = N_GROUPS_ATTN * group_cols
    xperm = cast_permute(x, dilations, bm=CAST_ROWS)
    planes = []
    bm, bn = PROJ_TILE
    for g in range(N_GROUPS_ATTN):
        first = g * group_cols // bn
        if g == 0:
            nqkv, gate_first = group_cols // bn, gate_col0 // bn
            n = group_cols + D_ATTN
            col_block = lambda j, nqkv=nqkv, gate_first=gate_first: jnp.where(
                j < nqkv, j, j - nqkv + gate_first)
        else:
            n = group_cols
            col_block = lambda j, first=first: first + j
        planes.append(proj_planes(xperm[g], w_in.astype(F32), col_block, n, D_ATTN,
                                  HEAD_DIM ** -0.5 * LOG2E, bm, bn))
    y = dilated_attention(planes, masked_bias(rel_bias), batch, seq)
    return outproj_layernorm(y, w_out.astype(BF16), x.reshape(batch * seq, d_model),
                             gamma, beta, alpha, *OUTPROJ_TILE_ATTN, emit_bf16=emit_bf16)


def _ssd_layer(x, xb, w_in, conv_w, conv_b, dt_bias, a_log, d_skip, norm_w, w_out,
               gamma, beta, alpha):
    batch, seq, d_model = x.shape
    m = batch * seq
    xf = x.reshape(m, d_model)
    proj, dt_t, da_t = proj_flat_dt(xb.reshape(m, d_model), w_in.astype(F32), dt_bias, a_log,
                                    *PROJ_TILE)
    y = ssd_mixer_core(proj, dt_t, da_t, conv_w, conv_b, d_skip, norm_w, batch, seq,
                       rows_per_step=min(SSD_ROWS, seq))
    return outproj_layernorm(y, w_out.astype(BF16), xf, gamma, beta, alpha,
                             *OUTPROJ_TILE_SSM, emit_bf16=False)


def kernel(x, w_in_attn, w_out_attn, rel_bias, w_in_ssm, conv_w, conv_b, dt_bias, a_log,
           d_skip, ssm_norm_w, w_out_ssm, ln_g, ln_b):
    batch, seq, d_model = x.shape
    depth = ln_g.shape[0]
    alpha = (2 * depth) ** 0.25
    xb = None
    for i in range(depth):
        j = i // 2
        if i % 2 == 0:
            outs = _attention_layer(x, w_in_attn[j], w_out_attn[j], rel_bias,
                                    ln_g[i], ln_b[i], alpha, emit_bf16=i + 1 < depth)
            if i + 1 < depth:
                xb = outs[1].reshape(batch, seq, d_model)
        else:
            outs = _ssd_layer(x, xb, w_in_ssm[j], conv_w[j], conv_b[j], dt_bias[j], a_log[j],
                              d_skip[j], ssm_norm_w[j], w_out_ssm[j], ln_g[i], ln_b[i], alpha)
        x = outs[0].reshape(batch, seq, d_model)
    return x
```
